```python
import math
import jax, jax.numpy as jnp
from jax import lax
import numpy as np

D_MODEL = 2048
BATCH = 2
SEQ = 4096
DEPTH = 1

SSM_EXPAND = 2
D_INNER = SSM_EXPAND * D_MODEL
SSM_HEAD_DIM = 64
SSM_HEADS = D_INNER // SSM_HEAD_DIM
SSM_GROUPS = 8
SSM_HEADS_PER_GROUP = SSM_HEADS // SSM_GROUPS
SSM_STATE = 128
SSM_CONV = 4
SSM_CONV_DIM = D_INNER + 2 * SSM_GROUPS * SSM_STATE
CHUNK = 128
SB_HEADS = 16
SB_HEAD_DIM = 128
SB_WIDTH = SB_HEADS * SB_HEAD_DIM
Q_BLOCK = 128
N_BRANCHES = 2
FFN_DIM = 5632
FFN_CONV = 3
EPS = 1e-6

_SPLIT_SIZES = [D_INNER, SSM_CONV_DIM, SSM_HEADS, SB_WIDTH, SB_WIDTH, SB_WIDTH, N_BRANCHES * D_MODEL]
D_IN_PROJ = sum(_SPLIT_SIZES)
SPLIT_IDX = [int(v) for v in np.cumsum(_SPLIT_SIZES)[:-1]]

kernel_name = "hybrid_ssd_stickbreaking_convffn"


def rms_norm(x, w):
    xf = x.astype(jnp.float32)
    y = xf * lax.rsqrt(jnp.mean(xf * xf, axis=-1, keepdims=True) + EPS)
    return (y * w.astype(jnp.float32)).astype(x.dtype)


def causal_dwconv(x, w, b):
    K = w.shape[0]
    S = x.shape[1]
    xp = jnp.pad(x, ((0, 0), (K - 1, 0), (0, 0)))
    y = b
    for k in range(K):
        y = y + w[k] * xp[:, k:k + S, :]
    return y


def ssd_chunked(x, dt, a_neg, Bm, Cm):
    b, l, h, p = x.shape
    G, E, N = SSM_GROUPS, SSM_HEADS_PER_GROUP, SSM_STATE
    nc = l // CHUNK
    dtype = x.dtype
    xc = x.reshape(b, nc, CHUNK, G, E, p)
    dtc = dt.reshape(b, nc, CHUNK, G, E)
    Bc = Bm.reshape(b, nc, CHUNK, G, N)
    Cc = Cm.reshape(b, nc, CHUNK, G, N)
    a = dtc.astype(jnp.float32) * a_neg.reshape(G, E).astype(jnp.float32)
    a = jnp.moveaxis(a, 2, -1)
    acs = jnp.cumsum(a, axis=-1)
    idx = jnp.arange(CHUNK)
    causal = idx[:, None] >= idx[None, :]
    seg = acs[..., :, None] - acs[..., None, :]
    Lmat = jnp.exp(jnp.where(causal, seg, -jnp.inf)).astype(dtype)
    xdt = xc * dtc[..., None]
    cb = jnp.einsum('bcqgn,bcsgn->bcgqs', Cc, Bc)
    y_diag = jnp.einsum('bcgqs,bcgeqs,bcsgep->bcqgep', cb, Lmat, xdt)
    decay_states = jnp.exp(acs[..., -1:] - acs).astype(dtype)
    states = jnp.einsum('bcsgn,bcges,bcsgep->bcgepn', Bc, decay_states, xdt)
    chunk_decay = jnp.exp(acs[..., -1]).astype(dtype)

    def step(hstate, inp):
        s_c, d_c = inp
        return hstate * d_c[..., None, None] + s_c, hstate

    h0 = jnp.zeros((b, G, E, p, N), dtype)
    _, prev = lax.scan(step, h0, (jnp.moveaxis(states, 1, 0), jnp.moveaxis(chunk_decay, 1, 0)))
    prev = jnp.moveaxis(prev, 0, 1)
    y_off = jnp.einsum('bcqgn,bcgepn,bcgeq->bcqgep', Cc, prev, jnp.exp(acs).astype(dtype))
    return (y_diag + y_off).reshape(b, l, h, p)


def stick_breaking_attention(q, k, v):
    b, h, S, d = q.shape
    nb = S // Q_BLOCK
    scale = 1.0 / math.sqrt(d)
    key_pos = jnp.arange(S)

    def block(i):
        start = i * Q_BLOCK
        qb = lax.dynamic_slice_in_dim(q, start, Q_BLOCK, axis=2)
        z = jnp.einsum('bhqd,bhkd->bhqk', qb, k).astype(jnp.float32) * scale
        q_pos = start + jnp.arange(Q_BLOCK)
        mask = key_pos[None, :] < q_pos[:, None]
        log_beta = jax.nn.log_sigmoid(z)
        log_keep = jnp.where(mask, jax.nn.log_sigmoid(-z), 0.0)
        later = lax.cumsum(log_keep, axis=log_keep.ndim - 1, reverse=True) - log_keep
        att = jnp.where(mask, jnp.exp(log_beta + later), 0.0)
        return jnp.einsum('bhqk,bhkd->bhqd', att.astype(v.dtype), v)

    out = lax.map(block, jnp.arange(nb))
    return jnp.moveaxis(out, 0, 2).reshape(b, h, S, d)


def setup_inputs(seed: int = 0) -> dict:
    key = jax.random.key(seed)
    ks = jax.random.split(key, 24)
    f32 = jnp.float32
    L = DEPTH

    def nrm(k, shape, scale):
        return jax.random.normal(k, shape, f32) * scale

    x = jax.random.normal(ks[0], (BATCH, SEQ, D_MODEL), f32)
    norm1_w = 1.0 + nrm(ks[1], (L, D_MODEL), 0.01)
    w_in = nrm(ks[2], (L, D_MODEL, D_IN_PROJ), D_MODEL ** -0.5)
    conv_ssm_w = nrm(ks[3], (L, SSM_CONV, SSM_CONV_DIM), SSM_CONV ** -0.5)
    conv_ssm_b = nrm(ks[4], (L, SSM_CONV_DIM), 0.01)
    dt0 = jnp.exp(jax.random.uniform(ks[5], (L, SSM_HEADS), f32, math.log(1e-3), math.log(1e-1)))
    dt_bias = dt0 + jnp.log(-jnp.expm1(-dt0))
    a_log = jnp.log(jax.random.uniform(ks[6], (L, SSM_HEADS), f32, 1.0, 16.0))
    d_skip = 1.0 + nrm(ks[7], (L, SSM_HEADS), 0.01)
    ssm_norm_w = 1.0 + nrm(ks[8], (L, D_INNER), 0.01)
    q_norm_w = 1.0 + nrm(ks[9], (L, SB_HEAD_DIM), 0.01)
    k_norm_w = 1.0 + nrm(ks[10], (L, SB_HEAD_DIM), 0.01)
    gate_b = nrm(ks[11], (L, N_BRANCHES * D_MODEL), 0.01)
    w_ssm_out = nrm(ks[12], (L, D_INNER, D_MODEL), D_INNER ** -0.5)
    w_att_out = nrm(ks[13], (L, SB_WIDTH, D_MODEL), SB_WIDTH ** -0.5)
    w_o = nrm(ks[14], (L, D_MODEL, D_MODEL), D_MODEL ** -0.5)
    norm2_w = 1.0 + nrm(ks[15], (L, D_MODEL), 0.01)
    w_up = nrm(ks[16], (L, D_MODEL, 2 * FFN_DIM), D_MODEL ** -0.5)
    conv_ffn_w = nrm(ks[17], (L, FFN_CONV, FFN_DIM), FFN_CONV ** -0.5)
    conv_ffn_b = nrm(ks[18], (L, FFN_DIM), 0.01)
    w_down = nrm(ks[19], (L, FFN_DIM, D_MODEL), FFN_DIM ** -0.5)
    return {"x": x, "norm1_w": norm1_w, "w_in": w_in, "conv_ssm_w": conv_ssm_w,
            "conv_ssm_b": conv_ssm_b, "dt_bias": dt_bias, "a_log": a_log, "d_skip": d_skip,
            "ssm_norm_w": ssm_norm_w, "q_norm_w": q_norm_w, "k_norm_w": k_norm_w,
            "gate_b": gate_b, "w_ssm_out": w_ssm_out, "w_att_out": w_att_out, "w_o": w_o,
            "norm2_w": norm2_w, "w_up": w_up, "conv_ffn_w": conv_ffn_w,
            "conv_ffn_b": conv_ffn_b, "w_down": w_down}


def reference(x, norm1_w, w_in, conv_ssm_w, conv_ssm_b, dt_bias, a_log, d_skip,
              ssm_norm_w, q_norm_w, k_norm_w, gate_b, w_ssm_out, w_att_out, w_o,
              norm2_w, w_up, conv_ffn_w, conv_ffn_b, w_down):
    b, S, _ = x.shape
    for l in range(DEPTH):
        u = rms_norm(x, norm1_w[l])
        proj = u @ w_in[l]
        z, xbc, dt_raw, q, k, v, gate_logits = jnp.split(proj, SPLIT_IDX, axis=-1)

        xbc = jax.nn.silu(causal_dwconv(xbc, conv_ssm_w[l], conv_ssm_b[l]))
        xs, Bm, Cm = jnp.split(xbc, [D_INNER, D_INNER + SSM_GROUPS * SSM_STATE], axis=-1)
        dt = jax.nn.softplus(dt_raw + dt_bias[l])
        a_neg = -jnp.exp(a_log[l])
        xs_h = xs.reshape(b, S, SSM_HEADS, SSM_HEAD_DIM)
        y = ssd_chunked(xs_h, dt, a_neg,
                        Bm.reshape(b, S, SSM_GROUPS, SSM_STATE),
                        Cm.reshape(b, S, SSM_GROUPS, SSM_STATE))
        y = y + d_skip[l][:, None] * xs_h
        y = y.reshape(b, S, D_INNER) * jax.nn.silu(z)
        y = rms_norm(y.reshape(b, S, SSM_GROUPS, D_INNER // SSM_GROUPS),
                     ssm_norm_w[l].reshape(SSM_GROUPS, D_INNER // SSM_GROUPS)).reshape(b, S, D_INNER)
        y_ssm = y @ w_ssm_out[l]

        qh = rms_norm(q.reshape(b, S, SB_HEADS, SB_HEAD_DIM), q_norm_w[l]).transpose(0, 2, 1, 3)
        kh = rms_norm(k.reshape(b, S, SB_HEADS, SB_HEAD_DIM), k_norm_w[l]).transpose(0, 2, 1, 3)
        vh = v.reshape(b, S, SB_HEADS, SB_HEAD_DIM).transpose(0, 2, 1, 3)
        att = stick_breaking_attention(qh, kh, vh)
        y_att = att.transpose(0, 2, 1, 3).reshape(b, S, SB_WIDTH) @ w_att_out[l]

        gates = jax.nn.sigmoid(gate_logits + gate_b[l]).reshape(b, S, N_BRANCHES, D_MODEL)
        mixed = gates[:, :, 0, :] * y_ssm + gates[:, :, 1, :] * y_att
        x = x + mixed @ w_o[l]

        u2 = rms_norm(x, norm2_w[l])
        up = u2 @ w_up[l]
        a_in, g_in = jnp.split(up, [FFN_DIM], axis=-1)
        a_in = causal_dwconv(a_in, conv_ffn_w[l], conv_ffn_b[l])
        x = x + (jax.nn.silu(a_in) * g_in) @ w_down[l]
    return x
```

```python
import functools
import math

import jax
import jax.numpy as jnp
from jax import lax
from jax.experimental import pallas as pl
from jax.experimental.pallas import tpu as pltpu

EPS = 1e-6
F32 = jnp.float32
BF16 = jnp.bfloat16

SSM_HEAD_DIM = 64
SSM_GROUPS = 8
SSM_STATE = 128
SSM_CONV = 4
CHUNK = 128
SB_HEAD_DIM = 128
FFN_CONV = 3

LANES = 128
SUBLANES = 8
VMEM_LIMIT_BYTES = 56 * 1024 * 1024


def _sigmoid(x):
    return 1.0 / (1.0 + jnp.exp(-x))


def _softplus(x):
    return jnp.maximum(x, 0.0) + jnp.log1p(jnp.exp(-jnp.abs(x)))


def _params(sem):
    return pltpu.CompilerParams(dimension_semantics=sem, vmem_limit_bytes=VMEM_LIMIT_BYTES)


def _rmsnorm_kernel(x_ref, w_ref, o_ref):
    x = x_ref[...]
    ms = jnp.mean(x * x, axis=-1, keepdims=True)
    o_ref[...] = (x * lax.rsqrt(ms + EPS) * w_ref[...]).astype(o_ref.dtype)


def _rmsnorm(x2d, w, tm):
    M, D = x2d.shape
    return pl.pallas_call(
        _rmsnorm_kernel,
        grid=(M // tm,),
        in_specs=[pl.BlockSpec((tm, D), lambda i: (i, 0)),
                  pl.BlockSpec((1, D), lambda i: (0, 0))],
        out_specs=pl.BlockSpec((tm, D), lambda i: (i, 0)),
        out_shape=jax.ShapeDtypeStruct((M, D), BF16),
        compiler_params=_params(("arbitrary",)),
        name="rmsnorm",
    )(x2d, w.reshape(1, D))


def _mm_kernel(*refs, n_mm, n_extra, epilogue):
    a_refs = refs[:n_mm]
    w_refs = refs[n_mm:2 * n_mm]
    extra = refs[2 * n_mm:2 * n_mm + n_extra]
    outs = refs[2 * n_mm + n_extra:]
    accs = [jnp.dot(a[...], w[...], preferred_element_type=F32) for a, w in zip(a_refs, w_refs)]
    epilogue(accs, extra, outs)


def _mm(name, a_list, w_list, extras, epilogue, outs, *, tm, tn):
    M = a_list[0].shape[0]
    N = w_list[0].shape[1]
    in_specs = []
    for a in a_list:
        in_specs.append(pl.BlockSpec((tm, a.shape[1]), lambda i, j: (i, 0)))
    for w in w_list:
        in_specs.append(pl.BlockSpec((w.shape[0], tn), lambda i, j: (0, j)))
    for _, bs, im in extras:
        in_specs.append(pl.BlockSpec(bs, im))
    out_specs = [pl.BlockSpec(bs, im) for _, _, bs, im in outs]
    out_shape = [jax.ShapeDtypeStruct(s, d) for s, d, _, _ in outs]
    res = pl.pallas_call(
        functools.partial(_mm_kernel, n_mm=len(a_list), n_extra=len(extras), epilogue=epilogue),
        grid=(M // tm, N // tn),
        in_specs=in_specs,
        out_specs=out_specs,
        out_shape=out_shape,
        compiler_params=_params(("arbitrary", "arbitrary")),
        name=name,
    )(*a_list, *w_list, *[e[0] for e in extras])
    return res


def _tile_map(i, j):
    return (i, j)


def _row_map(i, j):
    return (0, j)


def _ep_store(accs, extra, outs):
    outs[0][...] = accs[0].astype(outs[0].dtype)


def _ep_softplus_bias(accs, extra, outs):
    outs[0][...] = _softplus(accs[0] + extra[0][...])


def _ep_sigmoid_bias(accs, extra, outs):
    outs[0][...] = _sigmoid(accs[0] + extra[0][...]).astype(outs[0].dtype)


def _ep_head_norm(accs, extra, outs, *, scale):
    acc = accs[0]
    w = extra[0][...] * scale
    for c in range(acc.shape[1] // SB_HEAD_DIM):
        g = acc[:, c * SB_HEAD_DIM:(c + 1) * SB_HEAD_DIM]
        ms = jnp.mean(g * g, axis=-1, keepdims=True)
        outs[0][:, c * SB_HEAD_DIM:(c + 1) * SB_HEAD_DIM] = (g * lax.rsqrt(ms + EPS) * w).astype(outs[0].dtype)


def _ep_mix(accs, extra, outs):
    outs[0][...] = (extra[0][...].astype(F32) * accs[0] + extra[1][...].astype(F32) * accs[1]).astype(outs[0].dtype)


def _ep_residual_norm(accs, extra, outs):
    x1 = extra[0][...] + accs[0]
    outs[0][...] = x1
    ms = jnp.mean(x1 * x1, axis=-1, keepdims=True)
    outs[1][...] = (x1 * lax.rsqrt(ms + EPS) * extra[1][...]).astype(outs[1].dtype)


def _ep_residual(accs, extra, outs):
    outs[0][...] = extra[0][...] + accs[0]


def _ffn_up_kernel(u_ref, up_ref, wa_ref, wg_ref, cw_ref, cb_ref, o_ref, buf, *, tiles_per_seq):
    i = pl.program_id(0)
    tm = u_ref.shape[0]
    wa = wa_ref[...]
    a = jnp.dot(u_ref[...], wa, preferred_element_type=F32)
    g = jnp.dot(u_ref[...], wg_ref[...], preferred_element_type=F32)
    a_prev = jnp.dot(up_ref[...], wa, preferred_element_type=F32)
    first = (i % tiles_per_seq) == 0
    buf[0:SUBLANES, :] = jnp.where(first, 0.0, a_prev)
    buf[SUBLANES:SUBLANES + tm, :] = a
    y = cb_ref[...] + cw_ref[0:1, :] * buf[SUBLANES - 2:SUBLANES - 2 + tm, :]
    y = y + cw_ref[1:2, :] * buf[SUBLANES - 1:SUBLANES - 1 + tm, :]
    y = y + cw_ref[2:3, :] * a
    o_ref[...] = (y * _sigmoid(y) * g).astype(o_ref.dtype)


def _ffn_up(u2, w_up, conv_w, conv_b, seq, *, tm, tn):
    M, D = u2.shape
    F = w_up.shape[1] // 2
    nj = F // tn
    rb = tm // SUBLANES
    return pl.pallas_call(
        functools.partial(_ffn_up_kernel, tiles_per_seq=seq // tm),
        grid=(M // tm, nj),
        in_specs=[pl.BlockSpec((tm, D), lambda i, j: (i, 0)),
                  pl.BlockSpec((SUBLANES, D), lambda i, j: (jnp.maximum(i * rb - 1, 0), 0)),
                  pl.BlockSpec((D, tn), lambda i, j: (0, j)),
                  pl.BlockSpec((D, tn), lambda i, j: (0, j + nj)),
                  pl.BlockSpec((FFN_CONV, tn), lambda i, j: (0, j)),
                  pl.BlockSpec((1, tn), lambda i, j: (0, j))],
        out_specs=pl.BlockSpec((tm, tn), lambda i, j: (i, j)),
        out_shape=jax.ShapeDtypeStruct((M, F), BF16),
        scratch_shapes=[pltpu.VMEM((tm + SUBLANES, tn), F32)],
        compiler_params=_params(("arbitrary", "arbitrary")),
        name="ffn_up",
    )(u2, u2, w_up, w_up, conv_w, conv_b.reshape(1, F))


def _split3(x):
    h = x.astype(BF16)
    r = x - h.astype(F32)
    m = r.astype(BF16)
    l = (r - m.astype(F32)).astype(BF16)
    return h, m, l


def _ssd_kernel(xbc_ref, dt_ref, z_ref, cw_ref, cb_ref, alog_ref, dskip_ref, nw_ref, tri_ref,
                o_ref, buf, xact, st, ybuf, xd, *, d_inner, n_groups):
    Q = CHUNK
    P = SSM_HEAD_DIM
    NS = SSM_STATE
    gw = d_inner // n_groups
    hpg = gw // P
    c = pl.program_id(1)

    @pl.when(c == 0)
    def _():
        buf[0:SUBLANES, :] = jnp.zeros((SUBLANES, buf.shape[1]), F32)
        st[...] = jnp.zeros(st.shape, F32)

    conv_dim = xbc_ref.shape[1]
    slab = 512
    for j in range(conv_dim // slab):
        cs = slice(j * slab, (j + 1) * slab)
        buf[SUBLANES:SUBLANES + Q, cs] = xbc_ref[:, cs]
        acc = cb_ref[:, cs] + cw_ref[0:1, cs] * buf[SUBLANES - 3:SUBLANES - 3 + Q, cs]
        acc = acc + cw_ref[1:2, cs] * buf[SUBLANES - 2:SUBLANES - 2 + Q, cs]
        acc = acc + cw_ref[2:3, cs] * buf[SUBLANES - 1:SUBLANES - 1 + Q, cs]
        acc = acc + cw_ref[3:4, cs] * buf[SUBLANES:SUBLANES + Q, cs]
        xact[:, cs] = acc * _sigmoid(acc)
        buf[0:SUBLANES, cs] = buf[Q:Q + SUBLANES, cs]

    dt = dt_ref[...]
    a = dt * (-jnp.exp(alog_ref[...]))
    tri = tri_ref[...]
    ah, am, al = _split3(a)
    acs = (jnp.dot(tri, ah, preferred_element_type=F32)
           + jnp.dot(tri, am, preferred_element_type=F32)
           + jnp.dot(tri, al, preferred_element_type=F32))
    acs_t = acs.T
    dt_t = dt.T
    last = acs[Q - 1:Q, :]
    wdec = dt * jnp.exp(last - acs)
    qi = lax.broadcasted_iota(jnp.int32, (Q, Q), 0)
    si = lax.broadcasted_iota(jnp.int32, (Q, Q), 1)
    causal = qi >= si

    for g in range(n_groups):
        b_g = xact[:, d_inner + g * NS:d_inner + (g + 1) * NS]
        c_g = xact[:, d_inner + n_groups * NS + g * NS:d_inner + n_groups * NS + (g + 1) * NS]
        c_b = c_g.astype(BF16)
        cb = lax.dot_general(c_b, b_g.astype(BF16), (((1,), (1,)), ((), ())), preferred_element_type=F32)
        b_t = b_g.T.astype(BF16)
        gl = slice(g * gw, (g + 1) * gw)
        y_off = jnp.dot(c_b, st[:, gl].astype(BF16), preferred_element_type=F32)
        for e in range(hpg):
            h = g * hpg + e
            off = (e % 2) * P
            lo = g * gw + e * P
            colb = jnp.broadcast_to(acs[:, h:h + 1], (Q, Q))
            seg = jnp.where(causal, colb - acs_t[h:h + 1, :], -jnp.inf)
            m = (cb * jnp.exp(seg) * dt_t[h:h + 1, :]).astype(BF16)
            xh = xact[:, lo:lo + P]
            y_diag = jnp.dot(m, xh.astype(BF16), preferred_element_type=F32)
            eh = jnp.exp(colb[:, off:off + P])
            ybuf[:, lo:lo + P] = y_diag + eh * y_off[:, e * P:(e + 1) * P] + dskip_ref[:, lo:lo + P] * xh
            wb = jnp.broadcast_to(wdec[:, h:h + 1], (Q, Q))
            xd[:, e * P:(e + 1) * P] = (xh * wb[:, off:off + P]).astype(BF16)
            cd = jnp.exp(colb[Q - 1:Q, off:off + P])
            st[:, lo:lo + P] = st[:, lo:lo + P] * cd
        st[:, gl] = st[:, gl] + jnp.dot(b_t, xd[...], preferred_element_type=F32)

    for g in range(n_groups):
        gl = slice(g * gw, (g + 1) * gw)
        zz = z_ref[:, gl]
        yg = ybuf[:, gl] * (zz * _sigmoid(zz))
        ms = jnp.mean(yg * yg, axis=-1, keepdims=True)
        o_ref[:, gl] = (yg * lax.rsqrt(ms + EPS) * nw_ref[:, gl]).astype(o_ref.dtype)


def _ssd(xbc, dt, z, conv_w, conv_b, a_log_pad, dskip_row, norm_w, batch, seq):
    M, conv_dim = xbc.shape
    d_inner = z.shape[1]
    n_groups = SSM_GROUPS
    nc = seq // CHUNK
    tri = (jnp.arange(CHUNK)[:, None] >= jnp.arange(CHUNK)[None, :]).astype(BF16)
    row = lambda b, c: (b * nc + c, 0)
    fixed = lambda b, c: (0, 0)
    return pl.pallas_call(
        functools.partial(_ssd_kernel, d_inner=d_inner, n_groups=n_groups),
        grid=(batch, nc),
        in_specs=[pl.BlockSpec((CHUNK, conv_dim), row),
                  pl.BlockSpec((CHUNK, LANES), row),
                  pl.BlockSpec((CHUNK, d_inner), row),
                  pl.BlockSpec((SSM_CONV, conv_dim), fixed),
                  pl.BlockSpec((1, conv_dim), fixed),
                  pl.BlockSpec((1, LANES), fixed),
                  pl.BlockSpec((1, d_inner), fixed),
                  pl.BlockSpec((1, d_inner), fixed),
                  pl.BlockSpec((CHUNK, CHUNK), fixed)],
        out_specs=pl.BlockSpec((CHUNK, d_inner), row),
        out_shape=jax.ShapeDtypeStruct((M, d_inner), BF16),
        scratch_shapes=[pltpu.VMEM((CHUNK + SUBLANES, conv_dim), F32),
                        pltpu.VMEM((CHUNK, conv_dim), F32),
                        pltpu.VMEM((SSM_STATE, d_inner), F32),
                        pltpu.VMEM((CHUNK, d_inner), F32),
                        pltpu.VMEM((CHUNK, d_inner // n_groups), BF16)],
        compiler_params=_params(("arbitrary", "arbitrary")),
        name="ssd",
    )(xbc, dt, z, conv_w, conv_b.reshape(1, conv_dim), a_log_pad, dskip_row,
      norm_w.reshape(1, d_inner), tri)


def _attn_kernel(q_ref, k_ref, v_ref, tri_ref, o_ref, *, tq):
    i = pl.program_id(2)
    q = q_ref[...]
    tri = tri_ref[...]
    rows = lax.broadcasted_iota(jnp.int32, (tq, tq), 0)
    cols = lax.broadcasted_iota(jnp.int32, (tq, tq), 1)
    mask = cols < rows

    def block(kb, carry, diagonal):
        acc, run = carry
        ks = pl.multiple_of(kb * tq, tq)
        kblk = k_ref[pl.ds(ks, tq), :]
        vblk = v_ref[pl.ds(ks, tq), :]
        z = lax.dot_general(q, kblk, (((1,), (1,)), ((), ())), preferred_element_type=F32)
        sp = _softplus(z)
        lk = -sp
        if diagonal:
            lk = jnp.where(mask, lk, 0.0)
        hi = lk.astype(BF16)
        lo = (lk - hi.astype(F32)).astype(BF16)
        later = (jnp.dot(hi, tri, preferred_element_type=F32)
                 + jnp.dot(lo, tri, preferred_element_type=F32)) + run
        att = jnp.exp((z - sp) + later)
        if diagonal:
            att = jnp.where(mask, att, 0.0)
        acc = acc + jnp.dot(att.astype(BF16), vblk, preferred_element_type=F32)
        run = run + jnp.sum(lk, axis=-1, keepdims=True)
        return acc, run

    carry = (jnp.zeros((tq, SB_HEAD_DIM), F32), jnp.zeros((tq, 1), F32))
    carry = block(i, carry, True)
    carry = lax.fori_loop(0, i, lambda j, cr: block(i - 1 - j, cr, False), carry)
    o_ref[...] = carry[0].astype(o_ref.dtype)


def _attention(q, k, v, batch, seq, *, tq):
    M, W = q.shape
    heads = W // SB_HEAD_DIM
    nq = seq // tq
    tri = (jnp.arange(tq)[:, None] > jnp.arange(tq)[None, :]).astype(BF16)
    return pl.pallas_call(
        functools.partial(_attn_kernel, tq=tq),
        grid=(batch, heads, nq),
        in_specs=[pl.BlockSpec((tq, SB_HEAD_DIM), lambda b, h, i: (b * nq + i, h)),
                  pl.BlockSpec((seq, SB_HEAD_DIM), lambda b, h, i: (b, h)),
                  pl.BlockSpec((seq, SB_HEAD_DIM), lambda b, h, i: (b, h)),
                  pl.BlockSpec((tq, tq), lambda b, h, i: (0, 0))],
        out_specs=pl.BlockSpec((tq, SB_HEAD_DIM), lambda b, h, i: (b * nq + i, h)),
        out_shape=jax.ShapeDtypeStruct((M, W), BF16),
        compiler_params=_params(("arbitrary", "arbitrary", "arbitrary")),
        name="stickbreak_attn",
    )(q, k, v, tri)


def _tile(n, pref):
    t = pref
    while n % t:
        t //= 2
    return t


def _layer(x2d, batch, seq, norm1_w, w_in, conv_ssm_w, conv_ssm_b, dt_bias, a_log, d_skip, ssm_norm_w,
           q_norm_w, k_norm_w, gate_b, w_ssm_out, w_att_out, w_o, norm2_w, w_up, conv_ffn_w,
           conv_ffn_b, w_down):
    M, D = x2d.shape
    d_inner = w_ssm_out.shape[0]
    ssm_heads = dt_bias.shape[0]
    conv_dim = conv_ssm_w.shape[1]
    sb_width = w_att_out.shape[0]
    ffn = w_down.shape[0]
    tm = _tile(seq, 1024)

    o_z, o_xbc = 0, d_inner
    o_dt = o_xbc + conv_dim
    o_q = o_dt + ssm_heads
    o_k = o_q + sb_width
    o_v = o_k + sb_width
    o_g = o_v + sb_width
    w_z = w_in[:, o_z:o_xbc].astype(BF16)
    w_xbc = w_in[:, o_xbc:o_dt].astype(BF16)
    w_dt = jnp.pad(w_in[:, o_dt:o_q], ((0, 0), (0, LANES - ssm_heads))).astype(BF16)
    w_q = w_in[:, o_q:o_k].astype(BF16)
    w_k = w_in[:, o_k:o_v].astype(BF16)
    w_v = w_in[:, o_v:o_g].astype(BF16)
    w_g = w_in[:, o_g:].astype(BF16)
    pad_heads = lambda p: jnp.pad(p, (0, LANES - ssm_heads)).reshape(1, LANES)

    u = _rmsnorm(x2d, norm1_w, tm)

    def proj(name, w, epilogue, extras, dtype, tn_pref=1024):
        n = w.shape[1]
        tn = _tile(n, tn_pref)
        return _mm(name, [u], [w], extras, epilogue, [((M, n), dtype, (tm, tn), _tile_map)], tm=tm, tn=tn)[0]

    z = proj("proj_z", w_z, _ep_store, [], F32)
    xbc = proj("proj_xbc", w_xbc, _ep_store, [], F32)
    dt = proj("proj_dt", w_dt, _ep_softplus_bias, [(pad_heads(dt_bias), (1, LANES), _row_map)], F32)
    scale = 1.0 / math.sqrt(SB_HEAD_DIM)
    head_w = lambda w: (w.reshape(1, SB_HEAD_DIM), (1, SB_HEAD_DIM), lambda i, j: (0, 0))
    q = proj("proj_q", w_q, functools.partial(_ep_head_norm, scale=scale), [head_w(q_norm_w)], BF16)
    k = proj("proj_k", w_k, functools.partial(_ep_head_norm, scale=1.0), [head_w(k_norm_w)], BF16)
    v = proj("proj_v", w_v, _ep_store, [], BF16)
    tn_g = _tile(w_g.shape[1], 1024)
    gates = proj("proj_gate", w_g, _ep_sigmoid_bias,
                 [(gate_b.reshape(1, -1), (1, tn_g), _row_map)], F32)

    y_ssm_n = _ssd(xbc, dt, z, conv_ssm_w, conv_ssm_b, pad_heads(a_log),
                   jnp.repeat(d_skip, SSM_HEAD_DIM).reshape(1, d_inner), ssm_norm_w, batch, seq)
    att = _attention(q, k, v, batch, seq, tq=_tile(seq, 256))

    tn = _tile(D, 512)
    nb = D // tn
    tm_o = _tile(seq, 512)
    mixed = _mm("mix", [y_ssm_n, att], [w_ssm_out.astype(BF16), w_att_out.astype(BF16)],
                [(gates, (tm_o, tn), _tile_map), (gates, (tm_o, tn), lambda i, j: (i, j + nb))],
                _ep_mix, [((M, D), BF16, (tm_o, tn), _tile_map)], tm=tm_o, tn=tn)[0]

    x1, u2 = _mm("out_proj", [mixed], [w_o.astype(BF16)],
                 [(x2d, (tm_o, D), _tile_map), (norm2_w.reshape(1, D), (1, D), _row_map)],
                 _ep_residual_norm,
                 [((M, D), F32, (tm_o, D), _tile_map), ((M, D), BF16, (tm_o, D), _tile_map)],
                 tm=tm_o, tn=D)

    hdn = _ffn_up(u2, w_up.astype(BF16), conv_ffn_w, conv_ffn_b, seq, tm=tm, tn=_tile(ffn, 512))
    tn = _tile(D, 512)
    tm_d = _tile(seq, 512)
    out = _mm("ffn_down", [hdn], [w_down.astype(BF16)], [(x1, (tm_d, tn), _tile_map)], _ep_residual,
              [((M, D), F32, (tm_d, tn), _tile_map)], tm=tm_d, tn=tn)[0]
    return out


def kernel(x, norm1_w, w_in, conv_ssm_w, conv_ssm_b, dt_bias, a_log, d_skip, ssm_norm_w, q_norm_w,
           k_norm_w, gate_b, w_ssm_out, w_att_out, w_o, norm2_w, w_up, conv_ffn_w, conv_ffn_b, w_down):
    b, s, d = x.shape
    x2d = x.reshape(b * s, d)
    for l in range(norm1_w.shape[0]):
        x2d = _layer(x2d, b, s, norm1_w[l], w_in[l], conv_ssm_w[l], conv_ssm_b[l], dt_bias[l], a_log[l],
                     d_skip[l], ssm_norm_w[l], q_norm_w[l], k_norm_w[l], gate_b[l], w_ssm_out[l],
                     w_att_out[l], w_o[l], norm2_w[l], w_up[l], conv_ffn_w[l], conv_ffn_b[l], w_down[l])
    return x2d.reshape(b, s, d)
```

```python
import functools
import math

import jax
import jax.numpy as jnp
from jax import lax
from jax.experimental import pallas as pl
from jax.experimental.pallas import tpu as pltpu

EPS = 1e-6
F32 = jnp.float32
BF16 = jnp.bfloat16

SSM_HEAD_DIM = 64
SSM_GROUPS = 8
SSM_STATE = 128
SSM_CONV = 4
CHUNK = 128
SB_HEAD_DIM = 128
FFN_CONV = 3

LANES = 128
SUBLANES = 8
VMEM_LIMIT_BYTES = 56 * 1024 * 1024

F32_EXP_UNDERFLOW = 104.0


def _sigmoid(x):
    return 1.0 / (1.0 + jnp.exp(-x))


def _silu(x):
    return x * _sigmoid(x)


def _softplus(x):
    return jnp.maximum(x, 0.0) + jnp.log1p(jnp.exp(-jnp.abs(x)))


def _params(sem):
    return pltpu.CompilerParams(dimension_semantics=sem, vmem_limit_bytes=VMEM_LIMIT_BYTES)


def _rmsnorm_kernel(x_ref, w_ref, o_ref):
    x = x_ref[...]
    ms = jnp.mean(x * x, axis=-1, keepdims=True)
    o_ref[...] = (x * lax.rsqrt(ms + EPS) * w_ref[...]).astype(o_ref.dtype)


def _rmsnorm(x2d, w, tm):
    M, D = x2d.shape
    return pl.pallas_call(
        _rmsnorm_kernel,
        grid=(M // tm,),
        in_specs=[pl.BlockSpec((tm, D), lambda i: (i, 0)),
                  pl.BlockSpec((1, D), lambda i: (0, 0))],
        out_specs=pl.BlockSpec((tm, D), lambda i: (i, 0)),
        out_shape=jax.ShapeDtypeStruct((M, D), BF16),
        compiler_params=_params(("arbitrary",)),
        name="rmsnorm",
    )(x2d, w.reshape(1, D))


def _mm_kernel(*refs, n_mm, n_extra, n_out, cast, epilogue):
    a_refs = refs[:n_mm]
    w_refs = refs[n_mm:2 * n_mm]
    extra = refs[2 * n_mm:2 * n_mm + n_extra]
    outs = refs[2 * n_mm + n_extra:2 * n_mm + n_extra + n_out]
    scratch = list(refs[2 * n_mm + n_extra + n_out:])
    i = pl.program_id(1)
    w_bf = []
    for w, c in zip(w_refs, cast):
        w_bf.append(scratch.pop(0) if c else w)

    @pl.when(i == 0)
    def _():
        for w, wb, c in zip(w_refs, w_bf, cast):
            if c:
                wb[...] = w[...].astype(BF16)

    accs = [jnp.dot(a[...], wb[...], preferred_element_type=F32) for a, wb in zip(a_refs, w_bf)]
    epilogue(accs, extra, outs, scratch, i)


def _mm(name, a_list, w_list, extras, epilogue, outs, *, tm, tn, n_cols, scratch=()):
    M = a_list[0].shape[0]
    in_specs = []
    for a in a_list:
        in_specs.append(pl.BlockSpec((tm, a.shape[1]), lambda j, i: (i, 0)))
    cast = []
    scratch_shapes = []
    for w, off in w_list:
        in_specs.append(pl.BlockSpec((w.shape[0], tn), functools.partial(lambda j, i, off: (0, j + off), off=off)))
        cast.append(w.dtype != BF16)
        if cast[-1]:
            scratch_shapes.append(pltpu.VMEM((w.shape[0], tn), BF16))
    for _, bs, im in extras:
        in_specs.append(pl.BlockSpec(bs, im))
    out_specs = [pl.BlockSpec(bs, im) for _, _, bs, im in outs]
    out_shape = [jax.ShapeDtypeStruct(s, d) for s, d, _, _ in outs]
    return pl.pallas_call(
        functools.partial(_mm_kernel, n_mm=len(a_list), n_extra=len(extras), n_out=len(outs),
                          cast=tuple(cast), epilogue=epilogue),
        grid=(n_cols // tn, M // tm),
        in_specs=in_specs,
        out_specs=out_specs,
        out_shape=out_shape,
        scratch_shapes=scratch_shapes + list(scratch),
        compiler_params=_params(("arbitrary", "arbitrary")),
        name=name,
    )(*a_list, *[w for w, _ in w_list], *[e[0] for e in extras])


def _tile_map(j, i):
    return (i, j)


def _col_map(j, i):
    return (0, j)


def _fixed_map(j, i):
    return (0, 0)


def _ep_store(accs, extra, outs, scratch, i):
    outs[0][...] = accs[0].astype(outs[0].dtype)


def _ep_silu(accs, extra, outs, scratch, i):
    outs[0][...] = _silu(accs[0]).astype(outs[0].dtype)


def _ep_softplus_bias(accs, extra, outs, scratch, i):
    outs[0][...] = _softplus(accs[0] + extra[0][...])


def _ep_sigmoid_bias(accs, extra, outs, scratch, i):
    outs[0][...] = _sigmoid(accs[0] + extra[0][...]).astype(outs[0].dtype)


def _ep_head_norm(accs, extra, outs, scratch, i, *, scale):
    acc = accs[0]
    w = extra[0][...] * scale
    for c in range(acc.shape[1] // SB_HEAD_DIM):
        g = acc[:, c * SB_HEAD_DIM:(c + 1) * SB_HEAD_DIM]
        ms = jnp.mean(g * g, axis=-1, keepdims=True)
        outs[0][:, c * SB_HEAD_DIM:(c + 1) * SB_HEAD_DIM] = (g * lax.rsqrt(ms + EPS) * w).astype(outs[0].dtype)


def _causal_conv(acc, cw_ref, cb_ref, buf, i, tiles_per_seq):
    tm = acc.shape[0]
    taps = cw_ref.shape[0]

    @pl.when(i % tiles_per_seq == 0)
    def _():
        buf[0:SUBLANES, :] = jnp.zeros((SUBLANES, buf.shape[1]), F32)

    buf[SUBLANES:SUBLANES + tm, :] = acc
    y = cb_ref[...]
    for k in range(taps - 1):
        s = SUBLANES - (taps - 1) + k
        y = y + cw_ref[k:k + 1, :] * buf[s:s + tm, :]
    y = y + cw_ref[taps - 1:taps, :] * acc
    buf[0:SUBLANES, :] = buf[tm:tm + SUBLANES, :]
    return y


def _ep_conv_silu(accs, extra, outs, scratch, i, *, tiles_per_seq):
    y = _causal_conv(accs[0], extra[0], extra[1], scratch[0], i, tiles_per_seq)
    outs[0][...] = _silu(y).astype(outs[0].dtype)


def _ep_ffn_up(accs, extra, outs, scratch, i, *, tiles_per_seq):
    y = _causal_conv(accs[0], extra[0], extra[1], scratch[0], i, tiles_per_seq)
    outs[0][...] = (_silu(y) * accs[1]).astype(outs[0].dtype)


def _ep_mix(accs, extra, outs, scratch, i):
    outs[0][...] = (extra[0][...].astype(F32) * accs[0] + extra[1][...].astype(F32) * accs[1]).astype(outs[0].dtype)


def _ep_residual_norm(accs, extra, outs, scratch, i):
    x1 = extra[0][...] + accs[0]
    outs[0][...] = x1
    ms = jnp.mean(x1 * x1, axis=-1, keepdims=True)
    outs[1][...] = (x1 * lax.rsqrt(ms + EPS) * extra[1][...]).astype(outs[1].dtype)


def _ep_residual(accs, extra, outs, scratch, i):
    outs[0][...] = extra[0][...] + accs[0]


def _split3(x):
    h = x.astype(BF16)
    r = x - h.astype(F32)
    m = r.astype(BF16)
    l = (r - m.astype(F32)).astype(BF16)
    return h, m, l


def _ssd_kernel(xbc_ref, dt_ref, zs_ref, alog_ref, dskip_ref, nw_ref, tri_ref,
                o_ref, st, ybuf, xd, *, d_inner, n_groups):
    Q = CHUNK
    P = SSM_HEAD_DIM
    NS = SSM_STATE
    gw = d_inner // n_groups
    pairs = gw // (2 * P)
    c = pl.program_id(1)

    @pl.when(c == 0)
    def _():
        st[...] = jnp.zeros(st.shape, F32)

    dt = dt_ref[...]
    a = dt * (-jnp.exp(alog_ref[...]))
    tri = tri_ref[...]
    ah, am, al = _split3(a)
    acs = (jnp.dot(tri, ah, preferred_element_type=F32)
           + jnp.dot(tri, am, preferred_element_type=F32)
           + jnp.dot(tri, al, preferred_element_type=F32))
    acs_t = acs.T
    dt_t = dt.T
    wdec = dt * jnp.exp(acs[Q - 1:Q, :] - acs)
    qi = lax.broadcasted_iota(jnp.int32, (Q, Q), 0)
    si = lax.broadcasted_iota(jnp.int32, (Q, Q), 1)
    causal = qi >= si
    low = si < P
    low_b = jnp.where(low[0:1, :], 1.0, 0.0).astype(BF16)
    high_b = jnp.where(low[0:1, :], 0.0, 1.0).astype(BF16)

    for g in range(n_groups):
        b_g = xbc_ref[:, d_inner + g * NS:d_inner + (g + 1) * NS]
        c_g = xbc_ref[:, d_inner + (n_groups + g) * NS:d_inner + (n_groups + g + 1) * NS]
        cb = lax.dot_general(c_g, b_g, (((1,), (1,)), ((), ())), preferred_element_type=F32)
        b_t = b_g.astype(F32).T.astype(BF16)
        gl = slice(g * gw, (g + 1) * gw)
        y_off = jnp.dot(c_g, st[:, gl].astype(BF16), preferred_element_type=F32)
        for p in range(pairs):
            h0 = g * 2 * pairs + 2 * p
            pl_ = slice(g * gw + p * 2 * P, g * gw + (p + 1) * 2 * P)
            x_pair = xbc_ref[:, pl_]
            ms = []
            for h in (h0, h0 + 1):
                colb = jnp.broadcast_to(acs[:, h:h + 1], (Q, Q))
                seg = jnp.where(causal, colb - acs_t[h:h + 1, :], -jnp.inf)
                ms.append((cb * jnp.exp(seg) * dt_t[h:h + 1, :]).astype(BF16))
            colb0 = jnp.broadcast_to(acs[:, h0:h0 + 1], (Q, Q))
            colb1 = jnp.broadcast_to(acs[:, h0 + 1:h0 + 2], (Q, Q))
            x_bd = jnp.concatenate([x_pair * low_b, x_pair * high_b], axis=0)
            y_diag = jnp.dot(jnp.concatenate(ms, axis=1), x_bd, preferred_element_type=F32)
            e_pair = jnp.exp(jnp.where(low, colb0, colb1))
            x_f = x_pair.astype(F32)
            ybuf[:, pl_] = y_diag + e_pair * y_off[:, p * 2 * P:(p + 1) * 2 * P] + dskip_ref[:, pl_] * x_f
            wb0 = jnp.broadcast_to(wdec[:, h0:h0 + 1], (Q, Q))
            wb1 = jnp.broadcast_to(wdec[:, h0 + 1:h0 + 2], (Q, Q))
            xd[:, p * 2 * P:(p + 1) * 2 * P] = (x_f * jnp.where(low, wb0, wb1)).astype(BF16)
            st[:, pl_] = st[:, pl_] * e_pair[Q - 1:Q, :]
        st[:, gl] = st[:, gl] + jnp.dot(b_t, xd[...], preferred_element_type=F32)

    for g in range(n_groups):
        gl = slice(g * gw, (g + 1) * gw)
        yg = ybuf[:, gl] * zs_ref[:, gl].astype(F32)
        ms = jnp.mean(yg * yg, axis=-1, keepdims=True)
        o_ref[:, gl] = (yg * lax.rsqrt(ms + EPS) * nw_ref[:, gl]).astype(o_ref.dtype)


def _ssd(xbc, dt, zs, a_log_pad, dskip_row, norm_w, batch, seq):
    M, conv_dim = xbc.shape
    d_inner = zs.shape[1]
    n_groups = SSM_GROUPS
    nc = seq // CHUNK
    tri = (jnp.arange(CHUNK)[:, None] >= jnp.arange(CHUNK)[None, :]).astype(BF16)
    row = lambda b, c: (b * nc + c, 0)
    fixed = lambda b, c: (0, 0)
    return pl.pallas_call(
        functools.partial(_ssd_kernel, d_inner=d_inner, n_groups=n_groups),
        grid=(batch, nc),
        in_specs=[pl.BlockSpec((CHUNK, conv_dim), row),
                  pl.BlockSpec((CHUNK, LANES), row),
                  pl.BlockSpec((CHUNK, d_inner), row),
                  pl.BlockSpec((1, LANES), fixed),
                  pl.BlockSpec((1, d_inner), fixed),
                  pl.BlockSpec((1, d_inner), fixed),
                  pl.BlockSpec((CHUNK, CHUNK), fixed)],
        out_specs=pl.BlockSpec((CHUNK, d_inner), row),
        out_shape=jax.ShapeDtypeStruct((M, d_inner), BF16),
        scratch_shapes=[pltpu.VMEM((SSM_STATE, d_inner), F32),
                        pltpu.VMEM((CHUNK, d_inner), F32),
                        pltpu.VMEM((CHUNK, d_inner // n_groups), BF16)],
        compiler_params=_params(("arbitrary", "arbitrary")),
        name="ssd",
    )(xbc, dt, zs, a_log_pad, dskip_row, norm_w.reshape(1, d_inner), tri)


def _attn_kernel(q_ref, k_ref, v_ref, tri_ref, o_ref, *, tq):
    i = pl.program_id(2)
    q = q_ref[...]
    tri = tri_ref[...]
    rows = lax.broadcasted_iota(jnp.int32, (tq, tq), 0)
    cols = lax.broadcasted_iota(jnp.int32, (tq, tq), 1)
    mask = cols < rows

    def block(kb, run, diagonal):
        ks = pl.multiple_of(kb * tq, tq)
        kblk = k_ref[pl.ds(ks, tq), :]
        vblk = v_ref[pl.ds(ks, tq), :]
        z = lax.dot_general(q, kblk, (((1,), (1,)), ((), ())), preferred_element_type=F32)
        sp = _softplus(z)
        nlk = jnp.where(mask, sp, 0.0) if diagonal else sp
        hi = nlk.astype(BF16)
        lo = (nlk - hi.astype(F32)).astype(BF16)
        later = (jnp.dot(hi, tri, preferred_element_type=F32)
                 + jnp.dot(lo, tri, preferred_element_type=F32))
        att = jnp.exp(((z - sp) - later) - run)
        if diagonal:
            att = jnp.where(mask, att, 0.0)
        pv = jnp.dot(att.astype(BF16), vblk, preferred_element_type=F32)
        return pv, run + jnp.sum(nlk, axis=-1, keepdims=True)

    acc, run = block(i, jnp.zeros((tq, 1), F32), True)
    has_prev = i > 0
    pv, run2 = block(jnp.maximum(i - 1, 0), run, False)
    acc = acc + jnp.where(has_prev, pv, 0.0)
    run = jnp.where(has_prev, run2, run)

    def cond(c):
        kb, _, _, live = c
        return jnp.logical_and(kb >= 0, live)

    def body(c):
        kb, acc, run, _ = c
        pv, run = block(kb, run, False)
        return kb - 1, acc + pv, run, jnp.min(run) < F32_EXP_UNDERFLOW

    _, acc, _, _ = lax.while_loop(cond, body, (i - 2, acc, run, jnp.min(run) < F32_EXP_UNDERFLOW))
    o_ref[...] = acc.astype(o_ref.dtype)


def _attention(q, k, v, batch, seq, *, tq):
    M, W = q.shape
    heads = W // SB_HEAD_DIM
    nq = seq // tq
    tri = (jnp.arange(tq)[:, None] > jnp.arange(tq)[None, :]).astype(BF16)
    return pl.pallas_call(
        functools.partial(_attn_kernel, tq=tq),
        grid=(batch, heads, nq),
        in_specs=[pl.BlockSpec((tq, SB_HEAD_DIM), lambda b, h, i: (b * nq + i, h)),
                  pl.BlockSpec((seq, SB_HEAD_DIM), lambda b, h, i: (b, h)),
                  pl.BlockSpec((seq, SB_HEAD_DIM), lambda b, h, i: (b, h)),
                  pl.BlockSpec((tq, tq), lambda b, h, i: (0, 0))],
        out_specs=pl.BlockSpec((tq, SB_HEAD_DIM), lambda b, h, i: (b * nq + i, h)),
        out_shape=jax.ShapeDtypeStruct((M, W), BF16),
        compiler_params=_params(("arbitrary", "arbitrary", "arbitrary")),
        name="stickbreak_attn",
    )(q, k, v, tri)


def _tile(n, pref):
    t = pref
    while n % t:
        t //= 2
    return t


def _layer(x2d, batch, seq, norm1_w, w_in, conv_ssm_w, conv_ssm_b, dt_bias, a_log, d_skip, ssm_norm_w,
           q_norm_w, k_norm_w, gate_b, w_ssm_out, w_att_out, w_o, norm2_w, w_up, conv_ffn_w,
           conv_ffn_b, w_down):
    M, D = x2d.shape
    d_inner = w_ssm_out.shape[0]
    ssm_heads = dt_bias.shape[0]
    conv_dim = conv_ssm_w.shape[1]
    sb_width = w_att_out.shape[0]
    ffn = w_down.shape[0]
    tm = _tile(seq, 1024)
    tn = 1024

    o_dt = d_inner + conv_dim
    o_q = o_dt + ssm_heads
    assert d_inner % tn == 0 and conv_dim % tn == 0 and sb_width % tn == 0
    w_dt = jnp.pad(w_in[:, o_dt:o_q], ((0, 0), (0, LANES - ssm_heads))).astype(BF16)
    w_rest = w_in[:, o_q:].astype(BF16)
    pad_heads = lambda p: jnp.pad(p, (0, LANES - ssm_heads)).reshape(1, LANES)

    u = _rmsnorm(x2d, norm1_w, tm)

    def proj(name, w, off, n, epilogue, extras, dtype, tn=tn, scratch=()):
        return _mm(name, [u], [(w, off)], extras, epilogue, [((M, n), dtype, (tm, tn), _tile_map)],
                   tm=tm, tn=tn, n_cols=n, scratch=scratch)[0]

    zs = proj("proj_z", w_in, 0, d_inner, _ep_silu, [], BF16)
    xbc = proj("proj_xbc", w_in, d_inner // tn, conv_dim,
               functools.partial(_ep_conv_silu, tiles_per_seq=seq // tm),
               [(conv_ssm_w, (SSM_CONV, tn), _col_map), (conv_ssm_b.reshape(1, -1), (1, tn), _col_map)],
               BF16, scratch=[pltpu.VMEM((tm + SUBLANES, tn), F32)])
    dt = proj("proj_dt", w_dt, 0, LANES, _ep_softplus_bias,
              [(pad_heads(dt_bias), (1, LANES), _col_map)], F32, tn=LANES)
    scale = 1.0 / math.sqrt(SB_HEAD_DIM)
    head_w = lambda w: (w.reshape(1, SB_HEAD_DIM), (1, SB_HEAD_DIM), _fixed_map)
    nb = sb_width // tn
    q = proj("proj_q", w_rest, 0, sb_width, functools.partial(_ep_head_norm, scale=scale),
             [head_w(q_norm_w)], BF16)
    k = proj("proj_k", w_rest, nb, sb_width, functools.partial(_ep_head_norm, scale=1.0),
             [head_w(k_norm_w)], BF16)
    v = proj("proj_v", w_rest, 2 * nb, sb_width, _ep_store, [], BF16)
    gates = proj("proj_gate", w_rest, 3 * nb, 2 * D, _ep_sigmoid_bias,
                 [(gate_b.reshape(1, -1), (1, tn), _col_map)], BF16)

    y_ssm_n = _ssd(xbc, dt, zs, pad_heads(a_log),
                   jnp.repeat(d_skip, SSM_HEAD_DIM).reshape(1, d_inner), ssm_norm_w, batch, seq)
    att = _attention(q, k, v, batch, seq, tq=_tile(seq, 256))

    tn_s = 512
    tm_s = _tile(seq, 512)
    nbm = D // tn_s
    mixed = _mm("mix", [y_ssm_n, att], [(w_ssm_out, 0), (w_att_out, 0)],
                [(gates, (tm_s, tn_s), _tile_map), (gates, (tm_s, tn_s), lambda j, i: (i, j + nbm))],
                _ep_mix, [((M, D), BF16, (tm_s, tn_s), _tile_map)], tm=tm_s, tn=tn_s, n_cols=D)[0]

    x1, u2 = _mm("out_proj", [mixed], [(w_o.astype(BF16), 0)],
                 [(x2d, (tm_s, D), _tile_map), (norm2_w.reshape(1, D), (1, D), _col_map)],
                 _ep_residual_norm,
                 [((M, D), F32, (tm_s, D), _tile_map), ((M, D), BF16, (tm_s, D), _tile_map)],
                 tm=tm_s, tn=D, n_cols=D)

    hdn = _mm("ffn_up", [u2, u2], [(w_up, 0), (w_up, ffn // tn_s)],
              [(conv_ffn_w, (FFN_CONV, tn_s), _col_map), (conv_ffn_b.reshape(1, -1), (1, tn_s), _col_map)],
              functools.partial(_ep_ffn_up, tiles_per_seq=seq // tm),
              [((M, ffn), BF16, (tm, tn_s), _tile_map)], tm=tm, tn=tn_s, n_cols=ffn,
              scratch=[pltpu.VMEM((tm + SUBLANES, tn_s), F32)])[0]
    out = _mm("ffn_down", [hdn], [(w_down, 0)], [(x1, (tm_s, tn_s), _tile_map)], _ep_residual,
              [((M, D), F32, (tm_s, tn_s), _tile_map)], tm=tm_s, tn=tn_s, n_cols=D)[0]
    return out


def kernel(x, norm1_w, w_in, conv_ssm_w, conv_ssm_b, dt_bias, a_log, d_skip, ssm_norm_w, q_norm_w,
           k_norm_w, gate_b, w_ssm_out, w_att_out, w_o, norm2_w, w_up, conv_ffn_w, conv_ffn_b, w_down):
    b, s, d = x.shape
    x2d = x.reshape(b * s, d)
    for l in range(norm1_w.shape[0]):
        x2d = _layer(x2d, b, s, norm1_w[l], w_in[l], conv_ssm_w[l], conv_ssm_b[l], dt_bias[l], a_log[l],
                     d_skip[l], ssm_norm_w[l], q_norm_w[l], k_norm_w[l], gate_b[l], w_ssm_out[l],
                     w_att_out[l], w_o[l], norm2_w[l], w_up[l], conv_ffn_w[l], conv_ffn_b[l], w_down[l])
    return x2d.reshape(b, s, d)
```

```python
import functools
import math

import jax
import jax.numpy as jnp
from jax import lax
from jax.experimental import pallas as pl
from jax.experimental.pallas import tpu as pltpu

EPS = 1e-6
F32 = jnp.float32
BF16 = jnp.bfloat16

SSM_HEAD_DIM = 64
SSM_GROUPS = 8
SSM_STATE = 128
SSM_CONV = 4
CHUNK = 128
SB_HEAD_DIM = 128
FFN_CONV = 3

LANES = 128
SUBLANES = 8
VMEM_LIMIT_BYTES = 56 * 1024 * 1024

F32_EXP2_UNDERFLOW = 150.0


def _sigmoid(x):
    return 1.0 / (1.0 + jnp.exp(-x))


def _silu(x):
    return x * _sigmoid(x)


def _softplus(x):
    return jnp.maximum(x, 0.0) + jnp.log1p(jnp.exp(-jnp.abs(x)))


def _params(sem):
    return pltpu.CompilerParams(dimension_semantics=sem, vmem_limit_bytes=VMEM_LIMIT_BYTES)


def _rmsnorm_kernel(x_ref, w_ref, o_ref):
    x = x_ref[...]
    ms = jnp.mean(x * x, axis=-1, keepdims=True)
    o_ref[...] = (x * lax.rsqrt(ms + EPS) * w_ref[...]).astype(o_ref.dtype)


def _rmsnorm(x2d, w, tm):
    M, D = x2d.shape
    return pl.pallas_call(
        _rmsnorm_kernel,
        grid=(M // tm,),
        in_specs=[pl.BlockSpec((tm, D), lambda i: (i, 0)),
                  pl.BlockSpec((1, D), lambda i: (0, 0))],
        out_specs=pl.BlockSpec((tm, D), lambda i: (i, 0)),
        out_shape=jax.ShapeDtypeStruct((M, D), BF16),
        compiler_params=_params(("arbitrary",)),
        name="rmsnorm",
    )(x2d, w.reshape(1, D))


def _mm_kernel(*refs, n_mm, n_extra, n_out, stage, trans, rc, epilogue):
    a_refs = refs[:n_mm]
    w_refs = refs[n_mm:2 * n_mm]
    extra = refs[2 * n_mm:2 * n_mm + n_extra]
    outs = refs[2 * n_mm + n_extra:2 * n_mm + n_extra + n_out]
    scratch = list(refs[2 * n_mm + n_extra + n_out:])
    i = pl.program_id(1)
    w_bf = [scratch.pop(0) if s else w for w, s in zip(w_refs, stage)]

    @pl.when(i == 0)
    def _():
        for w, wb, s, t in zip(w_refs, w_bf, stage, trans):
            if s:
                wv = w[...].astype(BF16)
                wb[...] = wv.T if t else wv

    tm = a_refs[0].shape[0]
    for r in range(tm // rc):
        rs = slice(r * rc, (r + 1) * rc)
        accs = [jnp.dot(a[rs, :], wb[...], preferred_element_type=F32) for a, wb in zip(a_refs, w_bf)]
        epilogue(accs, extra, outs, scratch, i, rs)


def _mm(name, a_list, w_list, extras, epilogue, outs, *, tm, tn, n_cols, scratch=(), rc=128):
    M = a_list[0].shape[0]
    rc = min(rc, tm)
    in_specs = []
    for a in a_list:
        in_specs.append(pl.BlockSpec((tm, a.shape[1]), lambda j, i: (i, 0)))
    stage = []
    scratch_shapes = []
    for w, start, t in w_list:
        if t:
            k_dim = w.shape[1]
            in_specs.append(pl.BlockSpec((pl.Element(tn), pl.Element(k_dim)),
                                         functools.partial(
                                             lambda j, i, s: (pl.multiple_of(s + j * tn, SUBLANES), 0), s=start)))
        else:
            k_dim = w.shape[0]
            in_specs.append(pl.BlockSpec((k_dim, tn), functools.partial(lambda j, i, s: (0, j + s), s=start)))
        stage.append(t or w.dtype != BF16)
        if stage[-1]:
            scratch_shapes.append(pltpu.VMEM((k_dim, tn), BF16))
    for _, bs, im in extras:
        in_specs.append(pl.BlockSpec(bs, im))
    out_specs = [pl.BlockSpec(bs, im) for _, _, bs, im in outs]
    out_shape = [jax.ShapeDtypeStruct(s, d) for s, d, _, _ in outs]
    return pl.pallas_call(
        functools.partial(_mm_kernel, n_mm=len(a_list), n_extra=len(extras), n_out=len(outs),
                          stage=tuple(stage), trans=tuple(t for _, _, t in w_list), rc=rc, epilogue=epilogue),
        grid=(n_cols // tn, M // tm),
        in_specs=in_specs,
        out_specs=out_specs,
        out_shape=out_shape,
        scratch_shapes=scratch_shapes + list(scratch),
        compiler_params=_params(("arbitrary", "arbitrary")),
        name=name,
    )(*a_list, *[w[0] for w in w_list], *[e[0] for e in extras])


def _tile_map(j, i):
    return (i, j)


def _col_map(j, i):
    return (0, j)


def _fixed_map(j, i):
    return (0, 0)


def _ep_store(accs, extra, outs, scratch, i, rs):
    outs[0][rs, :] = accs[0].astype(outs[0].dtype)


def _ep_silu(accs, extra, outs, scratch, i, rs):
    outs[0][rs, :] = _silu(accs[0]).astype(outs[0].dtype)


def _ep_softplus_bias(accs, extra, outs, scratch, i, rs):
    outs[0][rs, :] = _softplus(accs[0] + extra[0][...])


def _ep_sigmoid_bias(accs, extra, outs, scratch, i, rs):
    outs[0][rs, :] = _sigmoid(accs[0] + extra[0][...]).astype(outs[0].dtype)


def _ep_head_norm(accs, extra, outs, scratch, i, rs, *, scale):
    acc = accs[0]
    w = extra[0][...] * scale
    for c in range(acc.shape[1] // SB_HEAD_DIM):
        g = acc[:, c * SB_HEAD_DIM:(c + 1) * SB_HEAD_DIM]
        ms = jnp.mean(g * g, axis=-1, keepdims=True)
        outs[0][rs, c * SB_HEAD_DIM:(c + 1) * SB_HEAD_DIM] = (g * lax.rsqrt(ms + EPS) * w).astype(outs[0].dtype)


def _causal_conv(acc, cw_ref, cb_ref, buf, i, rs, tiles_per_seq):
    taps = cw_ref.shape[0]
    tm = buf.shape[0] - SUBLANES
    if rs.start == 0:
        @pl.when(i % tiles_per_seq == 0)
        def _():
            buf[0:SUBLANES, :] = jnp.zeros((SUBLANES, buf.shape[1]), F32)

    buf[SUBLANES + rs.start:SUBLANES + rs.stop, :] = acc
    y = cb_ref[...]
    hist = buf[rs.start:rs.stop + SUBLANES, :]
    for k in range(taps - 1):
        y = y + cw_ref[k:k + 1, :] * pltpu.roll(hist, taps - 1 - k, 0)[SUBLANES:, :]
    y = y + cw_ref[taps - 1:taps, :] * acc
    if rs.stop == tm:
        buf[0:SUBLANES, :] = buf[tm:tm + SUBLANES, :]
    return y


def _ep_conv_silu(accs, extra, outs, scratch, i, rs, *, tiles_per_seq):
    y = _causal_conv(accs[0], extra[0], extra[1], scratch[0], i, rs, tiles_per_seq)
    outs[0][rs, :] = _silu(y).astype(outs[0].dtype)


def _ep_ffn_up(accs, extra, outs, scratch, i, rs, *, tiles_per_seq):
    y = _causal_conv(accs[0], extra[0], extra[1], scratch[0], i, rs, tiles_per_seq)
    outs[0][rs, :] = (_silu(y) * accs[1]).astype(outs[0].dtype)


def _ep_mix(accs, extra, outs, scratch, i, rs):
    g0 = extra[0][rs, :].astype(F32)
    g1 = extra[1][rs, :].astype(F32)
    outs[0][rs, :] = (g0 * accs[0] + g1 * accs[1]).astype(outs[0].dtype)


def _ep_residual_norm(accs, extra, outs, scratch, i, rs):
    x1 = extra[0][rs, :] + accs[0]
    outs[0][rs, :] = x1
    ms = jnp.mean(x1 * x1, axis=-1, keepdims=True)
    outs[1][rs, :] = (x1 * lax.rsqrt(ms + EPS) * extra[1][...]).astype(outs[1].dtype)


def _ep_residual(accs, extra, outs, scratch, i, rs):
    outs[0][rs, :] = extra[0][rs, :] + accs[0]


def _split3(x):
    h = x.astype(BF16)
    r = x - h.astype(F32)
    m = r.astype(BF16)
    l = (r - m.astype(F32)).astype(BF16)
    return h, m, l


def _ssd_kernel(xbc_ref, dt_ref, zs_ref, alog_ref, dskip_ref, nw_ref, tri_ref,
                o_ref, st, ybuf, xd, *, d_inner, n_groups):
    Q = CHUNK
    P = SSM_HEAD_DIM
    NS = SSM_STATE
    gw = d_inner // n_groups
    pairs = gw // (2 * P)
    c = pl.program_id(1)

    @pl.when(c == 0)
    def _():
        st[...] = jnp.zeros(st.shape, F32)

    dt = dt_ref[...]
    a = dt * (-jnp.exp(alog_ref[...]) * math.log2(math.e))
    tri = tri_ref[...]
    ah, am, al = _split3(a)
    acs = (jnp.dot(tri, ah, preferred_element_type=F32)
           + jnp.dot(tri, am, preferred_element_type=F32)
           + jnp.dot(tri, al, preferred_element_type=F32))
    src_t = (acs - jnp.log2(dt)).T
    wdec = dt * jnp.exp2(acs[Q - 1:Q, :] - acs)
    qi = lax.broadcasted_iota(jnp.int32, (Q, Q), 0)
    si = lax.broadcasted_iota(jnp.int32, (Q, Q), 1)
    causal = qi >= si
    low = si < P
    low_b = jnp.where(low[0:1, :], 1.0, 0.0).astype(BF16)
    high_b = jnp.where(low[0:1, :], 0.0, 1.0).astype(BF16)

    for g in range(n_groups):
        b_g = xbc_ref[:, d_inner + g * NS:d_inner + (g + 1) * NS]
        c_g = xbc_ref[:, d_inner + (n_groups + g) * NS:d_inner + (n_groups + g + 1) * NS]
        cb = lax.dot_general(c_g, b_g, (((1,), (1,)), ((), ())), preferred_element_type=F32)
        b_t = b_g.astype(F32).T.astype(BF16)
        gl = slice(g * gw, (g + 1) * gw)
        y_off = jnp.dot(c_g, st[:, gl].astype(BF16), preferred_element_type=F32)
        for p in range(pairs):
            h0 = g * 2 * pairs + 2 * p
            pl_ = slice(g * gw + p * 2 * P, g * gw + (p + 1) * 2 * P)
            x_pair = xbc_ref[:, pl_]
            ms = []
            for h in (h0, h0 + 1):
                colb = jnp.broadcast_to(acs[:, h:h + 1], (Q, Q))
                seg = jnp.where(causal, colb - src_t[h:h + 1, :], -jnp.inf)
                ms.append((cb * jnp.exp2(seg)).astype(BF16))
            colb0 = jnp.broadcast_to(acs[:, h0:h0 + 1], (Q, Q))
            colb1 = jnp.broadcast_to(acs[:, h0 + 1:h0 + 2], (Q, Q))
            x_bd = jnp.concatenate([x_pair * low_b, x_pair * high_b], axis=0)
            y_diag = jnp.dot(jnp.concatenate(ms, axis=1), x_bd, preferred_element_type=F32)
            e_pair = jnp.exp2(jnp.where(low, colb0, colb1))
            x_f = x_pair.astype(F32)
            ybuf[:, pl_] = y_diag + e_pair * y_off[:, p * 2 * P:(p + 1) * 2 * P] + dskip_ref[:, pl_] * x_f
            wb0 = jnp.broadcast_to(wdec[:, h0:h0 + 1], (Q, Q))
            wb1 = jnp.broadcast_to(wdec[:, h0 + 1:h0 + 2], (Q, Q))
            xd[:, p * 2 * P:(p + 1) * 2 * P] = (x_f * jnp.where(low, wb0, wb1)).astype(BF16)
            st[:, pl_] = st[:, pl_] * e_pair[Q - 1:Q, :]
        st[:, gl] = st[:, gl] + jnp.dot(b_t, xd[...], preferred_element_type=F32)

    for g in range(n_groups):
        gl = slice(g * gw, (g + 1) * gw)
        yg = ybuf[:, gl] * zs_ref[:, gl].astype(F32)
        ms = jnp.mean(yg * yg, axis=-1, keepdims=True)
        o_ref[:, gl] = (yg * lax.rsqrt(ms + EPS) * nw_ref[:, gl]).astype(o_ref.dtype)


def _ssd(xbc, dt, zs, a_log_pad, dskip_row, norm_w, batch, seq):
    M, conv_dim = xbc.shape
    d_inner = zs.shape[1]
    n_groups = SSM_GROUPS
    nc = seq // CHUNK
    tri = (jnp.arange(CHUNK)[:, None] >= jnp.arange(CHUNK)[None, :]).astype(BF16)
    row = lambda b, c: (b * nc + c, 0)
    fixed = lambda b, c: (0, 0)
    return pl.pallas_call(
        functools.partial(_ssd_kernel, d_inner=d_inner, n_groups=n_groups),
        grid=(batch, nc),
        in_specs=[pl.BlockSpec((CHUNK, conv_dim), row),
                  pl.BlockSpec((CHUNK, LANES), row),
                  pl.BlockSpec((CHUNK, d_inner), row),
                  pl.BlockSpec((1, LANES), fixed),
                  pl.BlockSpec((1, d_inner), fixed),
                  pl.BlockSpec((1, d_inner), fixed),
                  pl.BlockSpec((CHUNK, CHUNK), fixed)],
        out_specs=pl.BlockSpec((CHUNK, d_inner), row),
        out_shape=jax.ShapeDtypeStruct((M, d_inner), BF16),
        scratch_shapes=[pltpu.VMEM((SSM_STATE, d_inner), F32),
                        pltpu.VMEM((CHUNK, d_inner), F32),
                        pltpu.VMEM((CHUNK, d_inner // n_groups), BF16)],
        compiler_params=_params(("arbitrary", "arbitrary")),
        name="ssd",
    )(xbc, dt, zs, a_log_pad, dskip_row, norm_w.reshape(1, d_inner), tri)


def _attn_kernel(q_ref, k_ref, v_ref, tri_ref, o_ref, *, tq):
    i = pl.program_id(2)
    hb = q_ref.shape[1] // SB_HEAD_DIM
    tri = tri_ref[...]
    rows = lax.broadcasted_iota(jnp.int32, (tq, tq), 0)
    cols = lax.broadcasted_iota(jnp.int32, (tq, tq), 1)
    mask = cols < rows

    def block(h, kb, run, diagonal):
        ks = pl.multiple_of(kb * tq, tq)
        hs = slice(h * SB_HEAD_DIM, (h + 1) * SB_HEAD_DIM)
        kblk = k_ref[pl.ds(ks, tq), hs]
        vblk = v_ref[pl.ds(ks, tq), hs]
        z = lax.dot_general(q_ref[:, hs], kblk, (((1,), (1,)), ((), ())), preferred_element_type=F32)
        sp = jnp.maximum(z, 0.0) + jnp.log2(1.0 + jnp.exp2(-jnp.abs(z)))
        nlk = jnp.where(mask, sp, 0.0) if diagonal else sp
        later = jnp.dot(nlk.astype(BF16), tri, preferred_element_type=F32)
        att = jnp.exp2(((z - sp) - later) - run)
        if diagonal:
            att = jnp.where(mask, att, 0.0)
        pv = jnp.dot(att.astype(BF16), vblk, preferred_element_type=F32)
        return pv, run + jnp.sum(nlk, axis=-1, keepdims=True)

    def live(runs):
        return jnp.min(functools.reduce(jnp.minimum, runs)) < F32_EXP2_UNDERFLOW

    has_prev = i > 0
    accs, runs = [], []
    for h in range(hb):
        acc, run = block(h, i, jnp.zeros((tq, 1), F32), True)
        pv, run2 = block(h, jnp.maximum(i - 1, 0), run, False)
        accs.append(acc + jnp.where(has_prev, pv, 0.0))
        runs.append(jnp.where(has_prev, run2, run))

    def cond(c):
        return jnp.logical_and(c[0] >= 0, c[1])

    def body(c):
        kb, _, accs, runs = c
        new = [block(h, kb, runs[h], False) for h in range(hb)]
        accs = tuple(a + pv for a, (pv, _) in zip(accs, new))
        runs = tuple(r for _, r in new)
        return kb - 1, live(runs), accs, runs

    _, _, accs, _ = lax.while_loop(cond, body, (i - 2, live(runs), tuple(accs), tuple(runs)))
    for h in range(hb):
        o_ref[:, h * SB_HEAD_DIM:(h + 1) * SB_HEAD_DIM] = accs[h].astype(o_ref.dtype)


def _attention(q, k, v, batch, seq, *, tq, heads_per_step=2):
    M, W = q.shape
    hw = heads_per_step * SB_HEAD_DIM
    nq = seq // tq
    tri = (jnp.arange(tq)[:, None] > jnp.arange(tq)[None, :]).astype(BF16)
    return pl.pallas_call(
        functools.partial(_attn_kernel, tq=tq),
        grid=(batch, W // hw, nq),
        in_specs=[pl.BlockSpec((tq, hw), lambda b, h, i: (b * nq + i, h)),
                  pl.BlockSpec((seq, hw), lambda b, h, i: (b, h)),
                  pl.BlockSpec((seq, hw), lambda b, h, i: (b, h)),
                  pl.BlockSpec((tq, tq), lambda b, h, i: (0, 0))],
        out_specs=pl.BlockSpec((tq, hw), lambda b, h, i: (b * nq + i, h)),
        out_shape=jax.ShapeDtypeStruct((M, W), BF16),
        compiler_params=_params(("arbitrary", "arbitrary", "arbitrary")),
        name="stickbreak_attn",
    )(q, k, v, tri)


def _tile(n, pref):
    t = pref
    while n % t:
        t //= 2
    return t


def _layer(x2d, batch, seq, norm1_w, w_in, conv_ssm_w, conv_ssm_b, dt_bias, a_log, d_skip, ssm_norm_w,
           q_norm_w, k_norm_w, gate_b, w_ssm_out, w_att_out, w_o, norm2_w, w_up, conv_ffn_w,
           conv_ffn_b, w_down):
    M, D = x2d.shape
    d_inner = w_ssm_out.shape[0]
    ssm_heads = dt_bias.shape[0]
    conv_dim = conv_ssm_w.shape[1]
    sb_width = w_att_out.shape[0]
    ffn = w_down.shape[0]
    tm = _tile(seq, 1024)
    tn = 1024

    o_dt = d_inner + conv_dim
    o_q = o_dt + ssm_heads
    assert d_inner % tn == 0 and conv_dim % tn == 0 and sb_width % tn == 0 and o_q % SUBLANES == 0
    w_in_t = jnp.swapaxes(w_in, 0, 1)
    w_dt = jnp.pad(w_in_t[o_dt:o_q], ((0, LANES - ssm_heads), (0, 0))).astype(BF16)
    pad_heads = lambda p: jnp.pad(p, (0, LANES - ssm_heads)).reshape(1, LANES)

    u = _rmsnorm(x2d, norm1_w, tm)

    def proj(name, w, start, n, epilogue, extras, dtype, tn=tn, scratch=()):
        return _mm(name, [u], [(w, start, True)], extras, epilogue, [((M, n), dtype, (tm, tn), _tile_map)],
                   tm=tm, tn=tn, n_cols=n, scratch=scratch)[0]

    zs = proj("proj_z", w_in_t, 0, d_inner, _ep_silu, [], BF16)
    xbc = proj("proj_xbc", w_in_t, d_inner, conv_dim,
               functools.partial(_ep_conv_silu, tiles_per_seq=seq // tm),
               [(conv_ssm_w, (SSM_CONV, tn), _col_map), (conv_ssm_b.reshape(1, -1), (1, tn), _col_map)],
               BF16, scratch=[pltpu.VMEM((tm + SUBLANES, tn), F32)])
    dt = proj("proj_dt", w_dt, 0, LANES, _ep_softplus_bias,
              [(pad_heads(dt_bias), (1, LANES), _col_map)], F32, tn=LANES)
    scale = math.log2(math.e) / math.sqrt(SB_HEAD_DIM)
    head_w = lambda w: (w.reshape(1, SB_HEAD_DIM), (1, SB_HEAD_DIM), _fixed_map)
    q = proj("proj_q", w_in_t, o_q, sb_width, functools.partial(_ep_head_norm, scale=scale),
             [head_w(q_norm_w)], BF16)
    k = proj("proj_k", w_in_t, o_q + sb_width, sb_width, functools.partial(_ep_head_norm, scale=1.0),
             [head_w(k_norm_w)], BF16)
    v = proj("proj_v", w_in_t, o_q + 2 * sb_width, sb_width, _ep_store, [], BF16)
    gates = proj("proj_gate", w_in_t, o_q + 3 * sb_width, 2 * D, _ep_sigmoid_bias,
                 [(gate_b.reshape(1, -1), (1, tn), _col_map)], BF16)

    y_ssm_n = _ssd(xbc, dt, zs, pad_heads(a_log),
                   jnp.repeat(d_skip, SSM_HEAD_DIM).reshape(1, d_inner), ssm_norm_w, batch, seq)
    att = _attention(q, k, v, batch, seq, tq=_tile(seq, 256))

    tn_s = 512
    tm_s = _tile(seq, 512)
    nbm = D // tn_s
    mixed = _mm("mix", [y_ssm_n, att], [(w_ssm_out, 0, False), (w_att_out, 0, False)],
                [(gates, (tm_s, tn_s), _tile_map), (gates, (tm_s, tn_s), lambda j, i: (i, j + nbm))],
                _ep_mix, [((M, D), BF16, (tm_s, tn_s), _tile_map)], tm=tm_s, tn=tn_s, n_cols=D)[0]

    x1, u2 = _mm("out_proj", [mixed], [(w_o.astype(BF16), 0, False)],
                 [(x2d, (tm_s, D), _tile_map), (norm2_w.reshape(1, D), (1, D), _col_map)],
                 _ep_residual_norm,
                 [((M, D), F32, (tm_s, D), _tile_map), ((M, D), BF16, (tm_s, D), _tile_map)],
                 tm=tm_s, tn=D, n_cols=D)

    hdn = _mm("ffn_up", [u2, u2], [(w_up, 0, False), (w_up, ffn // tn_s, False)],
              [(conv_ffn_w, (FFN_CONV, tn_s), _col_map), (conv_ffn_b.reshape(1, -1), (1, tn_s), _col_map)],
              functools.partial(_ep_ffn_up, tiles_per_seq=seq // tm),
              [((M, ffn), BF16, (tm, tn_s), _tile_map)], tm=tm, tn=tn_s, n_cols=ffn,
              scratch=[pltpu.VMEM((tm + SUBLANES, tn_s), F32)])[0]
    out = _mm("ffn_down", [hdn], [(w_down, 0, False)], [(x1, (tm_s, tn_s), _tile_map)], _ep_residual,
              [((M, D), F32, (tm_s, tn_s), _tile_map)], tm=tm_s, tn=tn_s, n_cols=D)[0]
    return out


def kernel(x, norm1_w, w_in, conv_ssm_w, conv_ssm_b, dt_bias, a_log, d_skip, ssm_norm_w, q_norm_w,
           k_norm_w, gate_b, w_ssm_out, w_att_out, w_o, norm2_w, w_up, conv_ffn_w, conv_ffn_b, w_down):
    b, s, d = x.shape
    x2d = x.reshape(b * s, d)
    for l in range(norm1_w.shape[0]):
        x2d = _layer(x2d, b, s, norm1_w[l], w_in[l], conv_ssm_w[l], conv_ssm_b[l], dt_bias[l], a_log[l],
                     d_skip[l], ssm_norm_w[l], q_norm_w[l], k_norm_w[l], gate_b[l], w_ssm_out[l],
                     w_att_out[l], w_o[l], norm2_w[l], w_up[l], conv_ffn_w[l], conv_ffn_b[l], w_down[l])
    return x2d.reshape(b, s, d)
```

```python
import functools
import math

import jax
import jax.numpy as jnp
from jax import lax
from jax.experimental import pallas as pl
from jax.experimental.pallas import tpu as pltpu

EPS = 1e-6
F32 = jnp.float32
BF16 = jnp.bfloat16

SSM_HEAD_DIM = 64
SSM_GROUPS = 8
SSM_STATE = 128
SSM_CONV = 4
CHUNK = 128
SB_HEAD_DIM = 128
FFN_CONV = 3

LANES = 128
SUBLANES = 8
VMEM_LIMIT_BYTES = 56 * 1024 * 1024

F32_EXP2_UNDERFLOW = 150.0


def _sigmoid(x):
    return 1.0 / (1.0 + jnp.exp(-x))


def _silu(x):
    return x * _sigmoid(x)


def _softplus(x):
    return jnp.maximum(x, 0.0) + jnp.log1p(jnp.exp(-jnp.abs(x)))


def _params(sem):
    return pltpu.CompilerParams(dimension_semantics=sem, vmem_limit_bytes=VMEM_LIMIT_BYTES)


def _rmsnorm_kernel(x_ref, w_ref, o_ref):
    x = x_ref[...]
    ms = jnp.mean(x * x, axis=-1, keepdims=True)
    o_ref[...] = (x * lax.rsqrt(ms + EPS) * w_ref[...]).astype(o_ref.dtype)


def _rmsnorm(x2d, w, tm):
    M, D = x2d.shape
    return pl.pallas_call(
        _rmsnorm_kernel,
        grid=(M // tm,),
        in_specs=[pl.BlockSpec((tm, D), lambda i: (i, 0)),
                  pl.BlockSpec((1, D), lambda i: (0, 0))],
        out_specs=pl.BlockSpec((tm, D), lambda i: (i, 0)),
        out_shape=jax.ShapeDtypeStruct((M, D), BF16),
        compiler_params=_params(("arbitrary",)),
        name="rmsnorm",
    )(x2d, w.reshape(1, D))


def _mm_kernel(*refs, n_mm, n_extra, n_out, stage, trans, rc, epilogue):
    a_refs = refs[:n_mm]
    w_refs = refs[n_mm:2 * n_mm]
    extra = refs[2 * n_mm:2 * n_mm + n_extra]
    outs = refs[2 * n_mm + n_extra:2 * n_mm + n_extra + n_out]
    scratch = list(refs[2 * n_mm + n_extra + n_out:])
    i = pl.program_id(1)
    w_bf = [scratch.pop(0) if s else w for w, s in zip(w_refs, stage)]

    @pl.when(i == 0)
    def _():
        for w, wb, s, t in zip(w_refs, w_bf, stage, trans):
            if s:
                wv = w[...].astype(BF16)
                wb[...] = wv.T if t else wv

    tm = a_refs[0].shape[0]
    carry = {}
    for r in range(tm // rc):
        rs = slice(r * rc, (r + 1) * rc)
        accs = [jnp.dot(a[rs, :], wb[...], preferred_element_type=F32) for a, wb in zip(a_refs, w_bf)]
        epilogue(accs, extra, outs, scratch, i, rs, carry)


def _mm(name, a_list, w_list, extras, epilogue, outs, *, tm, tn, n_cols, scratch=(), rc=128):
    M = a_list[0].shape[0]
    rc = min(rc, tm)
    in_specs = []
    for a in a_list:
        in_specs.append(pl.BlockSpec((tm, a.shape[1]), lambda j, i: (i, 0)))
    stage = []
    scratch_shapes = []
    for w, start, t in w_list:
        if t:
            k_dim = w.shape[1]
            in_specs.append(pl.BlockSpec((pl.Element(tn), pl.Element(k_dim)),
                                         functools.partial(
                                             lambda j, i, s: (pl.multiple_of(s + j * tn, SUBLANES), 0), s=start)))
        else:
            k_dim = w.shape[0]
            in_specs.append(pl.BlockSpec((k_dim, tn), functools.partial(lambda j, i, s: (0, j + s), s=start)))
        stage.append(t or w.dtype != BF16)
        if stage[-1]:
            scratch_shapes.append(pltpu.VMEM((k_dim, tn), BF16))
    for _, bs, im in extras:
        in_specs.append(pl.BlockSpec(bs, im))
    out_specs = [pl.BlockSpec(bs, im) for _, _, bs, im in outs]
    out_shape = [jax.ShapeDtypeStruct(s, d) for s, d, _, _ in outs]
    return pl.pallas_call(
        functools.partial(_mm_kernel, n_mm=len(a_list), n_extra=len(extras), n_out=len(outs),
                          stage=tuple(stage), trans=tuple(t for _, _, t in w_list), rc=rc, epilogue=epilogue),
        grid=(n_cols // tn, M // tm),
        in_specs=in_specs,
        out_specs=out_specs,
        out_shape=out_shape,
        scratch_shapes=scratch_shapes + list(scratch),
        compiler_params=_params(("arbitrary", "arbitrary")),
        name=name,
    )(*a_list, *[w[0] for w in w_list], *[e[0] for e in extras])


def _tile_map(j, i):
    return (i, j)


def _col_map(j, i):
    return (0, j)


def _fixed_map(j, i):
    return (0, 0)


def _ep_store(accs, extra, outs, scratch, i, rs, carry):
    outs[0][rs, :] = accs[0].astype(outs[0].dtype)


def _ep_silu(accs, extra, outs, scratch, i, rs, carry):
    outs[0][rs, :] = _silu(accs[0]).astype(outs[0].dtype)


def _ep_softplus_bias(accs, extra, outs, scratch, i, rs, carry):
    outs[0][rs, :] = _softplus(accs[0] + extra[0][...])


def _ep_sigmoid_bias(accs, extra, outs, scratch, i, rs, carry):
    outs[0][rs, :] = _sigmoid(accs[0] + extra[0][...]).astype(outs[0].dtype)


def _ep_head_norm(accs, extra, outs, scratch, i, rs, carry, *, scale):
    acc = accs[0]
    w = extra[0][...] * scale
    for c in range(acc.shape[1] // SB_HEAD_DIM):
        g = acc[:, c * SB_HEAD_DIM:(c + 1) * SB_HEAD_DIM]
        ms = jnp.mean(g * g, axis=-1, keepdims=True)
        outs[0][rs, c * SB_HEAD_DIM:(c + 1) * SB_HEAD_DIM] = (g * lax.rsqrt(ms + EPS) * w).astype(outs[0].dtype)


def _causal_conv(acc, cw_ref, cb_ref, tail_ref, i, rs, carry, tm, tiles_per_seq):
    taps = cw_ref.shape[0]
    if rs.start == 0:
        tail = jnp.where(i % tiles_per_seq == 0, 0.0, tail_ref[...])
    else:
        tail = carry["tail"]
    hist = jnp.concatenate([tail, acc], axis=0)
    y = cb_ref[...]
    for k in range(taps - 1):
        y = y + cw_ref[k:k + 1, :] * pltpu.roll(hist, taps - 1 - k, 0)[SUBLANES:, :]
    y = y + cw_ref[taps - 1:taps, :] * acc
    carry["tail"] = acc[acc.shape[0] - SUBLANES:, :]
    if rs.stop == tm:
        tail_ref[...] = carry["tail"]
    return y


def _ep_conv_silu(accs, extra, outs, scratch, i, rs, carry, *, tiles_per_seq):
    y = _causal_conv(accs[0], extra[0], extra[1], scratch[0], i, rs, carry, outs[0].shape[0], tiles_per_seq)
    outs[0][rs, :] = _silu(y).astype(outs[0].dtype)


def _ep_ffn_up(accs, extra, outs, scratch, i, rs, carry, *, tiles_per_seq):
    y = _causal_conv(accs[0], extra[0], extra[1], scratch[0], i, rs, carry, outs[0].shape[0], tiles_per_seq)
    outs[0][rs, :] = (_silu(y) * accs[1]).astype(outs[0].dtype)


def _ep_mix(accs, extra, outs, scratch, i, rs, carry):
    g0 = extra[0][rs, :].astype(F32)
    g1 = extra[1][rs, :].astype(F32)
    outs[0][rs, :] = (g0 * accs[0] + g1 * accs[1]).astype(outs[0].dtype)


def _ep_residual_norm(accs, extra, outs, scratch, i, rs, carry):
    x1 = extra[0][rs, :] + accs[0]
    outs[0][rs, :] = x1
    ms = jnp.mean(x1 * x1, axis=-1, keepdims=True)
    outs[1][rs, :] = (x1 * lax.rsqrt(ms + EPS) * extra[1][...]).astype(outs[1].dtype)


def _ep_residual(accs, extra, outs, scratch, i, rs, carry):
    outs[0][rs, :] = extra[0][rs, :] + accs[0]


def _split3(x):
    h = x.astype(BF16)
    r = x - h.astype(F32)
    m = r.astype(BF16)
    l = (r - m.astype(F32)).astype(BF16)
    return h, m, l


def _ssd_kernel(xbc_ref, dt_ref, zs_ref, alog_ref, dskip_ref, nw_ref, tri_ref,
                o_ref, st, ybuf, xd, *, d_inner, n_groups):
    Q = CHUNK
    P = SSM_HEAD_DIM
    NS = SSM_STATE
    gw = d_inner // n_groups
    pairs = gw // (2 * P)
    c = pl.program_id(1)

    @pl.when(c == 0)
    def _():
        st[...] = jnp.zeros(st.shape, F32)

    dt = dt_ref[...]
    a = dt * (-jnp.exp(alog_ref[...]) * math.log2(math.e))
    tri = tri_ref[...]
    ah, am, al = _split3(a)
    acs = (jnp.dot(tri, ah, preferred_element_type=F32)
           + jnp.dot(tri, am, preferred_element_type=F32)
           + jnp.dot(tri, al, preferred_element_type=F32))
    src_t = (acs - jnp.log2(dt)).T
    wdec = dt * jnp.exp2(acs[Q - 1:Q, :] - acs)
    qi = lax.broadcasted_iota(jnp.int32, (Q, Q), 0)
    si = lax.broadcasted_iota(jnp.int32, (Q, Q), 1)
    causal = qi >= si
    low = si < P
    low_b = jnp.where(low[0:1, :], 1.0, 0.0).astype(BF16)
    high_b = jnp.where(low[0:1, :], 0.0, 1.0).astype(BF16)

    for g in range(n_groups):
        b_g = xbc_ref[:, d_inner + g * NS:d_inner + (g + 1) * NS]
        c_g = xbc_ref[:, d_inner + (n_groups + g) * NS:d_inner + (n_groups + g + 1) * NS]
        cb = lax.dot_general(c_g, b_g, (((1,), (1,)), ((), ())), preferred_element_type=F32)
        b_t = b_g.astype(F32).T.astype(BF16)
        gl = slice(g * gw, (g + 1) * gw)
        y_off = jnp.dot(c_g, st[:, gl].astype(BF16), preferred_element_type=F32)
        for p in range(pairs):
            h0 = g * 2 * pairs + 2 * p
            pl_ = slice(g * gw + p * 2 * P, g * gw + (p + 1) * 2 * P)
            x_pair = xbc_ref[:, pl_]
            ms = []
            for h in (h0, h0 + 1):
                colb = jnp.broadcast_to(acs[:, h:h + 1], (Q, Q))
                seg = jnp.where(causal, colb - src_t[h:h + 1, :], -jnp.inf)
                ms.append((cb * jnp.exp2(seg)).astype(BF16))
            colb0 = jnp.broadcast_to(acs[:, h0:h0 + 1], (Q, Q))
            colb1 = jnp.broadcast_to(acs[:, h0 + 1:h0 + 2], (Q, Q))
            x_bd = jnp.concatenate([x_pair * low_b, x_pair * high_b], axis=0)
            y_diag = jnp.dot(jnp.concatenate(ms, axis=1), x_bd, preferred_element_type=F32)
            e_pair = jnp.exp2(jnp.where(low, colb0, colb1))
            x_f = x_pair.astype(F32)
            ybuf[:, pl_] = y_diag + e_pair * y_off[:, p * 2 * P:(p + 1) * 2 * P] + dskip_ref[:, pl_] * x_f
            wb0 = jnp.broadcast_to(wdec[:, h0:h0 + 1], (Q, Q))
            wb1 = jnp.broadcast_to(wdec[:, h0 + 1:h0 + 2], (Q, Q))
            xd[:, p * 2 * P:(p + 1) * 2 * P] = (x_f * jnp.where(low, wb0, wb1)).astype(BF16)
            st[:, pl_] = st[:, pl_] * e_pair[Q - 1:Q, :]
        st[:, gl] = st[:, gl] + jnp.dot(b_t, xd[...], preferred_element_type=F32)

    for g in range(n_groups):
        gl = slice(g * gw, (g + 1) * gw)
        yg = ybuf[:, gl] * zs_ref[:, gl].astype(F32)
        ms = jnp.mean(yg * yg, axis=-1, keepdims=True)
        o_ref[:, gl] = (yg * lax.rsqrt(ms + EPS) * nw_ref[:, gl]).astype(o_ref.dtype)


def _ssd(xbc, dt, zs, a_log_pad, dskip_row, norm_w, batch, seq):
    M, conv_dim = xbc.shape
    d_inner = zs.shape[1]
    n_groups = SSM_GROUPS
    nc = seq // CHUNK
    tri = (jnp.arange(CHUNK)[:, None] >= jnp.arange(CHUNK)[None, :]).astype(BF16)
    row = lambda b, c: (b * nc + c, 0)
    fixed = lambda b, c: (0, 0)
    return pl.pallas_call(
        functools.partial(_ssd_kernel, d_inner=d_inner, n_groups=n_groups),
        grid=(batch, nc),
        in_specs=[pl.BlockSpec((CHUNK, conv_dim), row),
                  pl.BlockSpec((CHUNK, LANES), row),
                  pl.BlockSpec((CHUNK, d_inner), row),
                  pl.BlockSpec((1, LANES), fixed),
                  pl.BlockSpec((1, d_inner), fixed),
                  pl.BlockSpec((1, d_inner), fixed),
                  pl.BlockSpec((CHUNK, CHUNK), fixed)],
        out_specs=pl.BlockSpec((CHUNK, d_inner), row),
        out_shape=jax.ShapeDtypeStruct((M, d_inner), BF16),
        scratch_shapes=[pltpu.VMEM((SSM_STATE, d_inner), F32),
                        pltpu.VMEM((CHUNK, d_inner), F32),
                        pltpu.VMEM((CHUNK, d_inner // n_groups), BF16)],
        compiler_params=_params(("arbitrary", "arbitrary")),
        name="ssd",
    )(xbc, dt, zs, a_log_pad, dskip_row, norm_w.reshape(1, d_inner), tri)


def _attn_kernel(q_ref, k_ref, v_ref, tri_ref, o_ref, *, tq):
    i = pl.program_id(2)
    hb = q_ref.shape[1] // SB_HEAD_DIM
    tri = tri_ref[...]
    rows = lax.broadcasted_iota(jnp.int32, (tq, tq), 0)
    cols = lax.broadcasted_iota(jnp.int32, (tq, tq), 1)
    mask = cols < rows

    def block(h, kb, run, diagonal):
        ks = pl.multiple_of(kb * tq, tq)
        hs = slice(h * SB_HEAD_DIM, (h + 1) * SB_HEAD_DIM)
        kblk = k_ref[pl.ds(ks, tq), hs]
        vblk = v_ref[pl.ds(ks, tq), hs]
        z = lax.dot_general(q_ref[:, hs], kblk, (((1,), (1,)), ((), ())), preferred_element_type=F32)
        sp = jnp.maximum(z, 0.0) + jnp.log2(1.0 + jnp.exp2(-jnp.abs(z)))
        nlk = jnp.where(mask, sp, 0.0) if diagonal else sp
        later = jnp.dot(nlk.astype(BF16), tri, preferred_element_type=F32)
        att = jnp.exp2(((z - sp) - later) - run)
        if diagonal:
            att = jnp.where(mask, att, 0.0)
        pv = jnp.dot(att.astype(BF16), vblk, preferred_element_type=F32)
        return pv, run + jnp.sum(nlk, axis=-1, keepdims=True)

    def live(runs):
        return jnp.min(functools.reduce(jnp.minimum, runs)) < F32_EXP2_UNDERFLOW

    has_prev = i > 0
    accs, runs = [], []
    for h in range(hb):
        acc, run = block(h, i, jnp.zeros((tq, 1), F32), True)
        pv, run2 = block(h, jnp.maximum(i - 1, 0), run, False)
        accs.append(acc + jnp.where(has_prev, pv, 0.0))
        runs.append(jnp.where(has_prev, run2, run))

    def cond(c):
        return jnp.logical_and(c[0] >= 0, c[1])

    def body(c):
        kb, _, accs, runs = c
        new = [block(h, kb, runs[h], False) for h in range(hb)]
        accs = tuple(a + pv for a, (pv, _) in zip(accs, new))
        runs = tuple(r for _, r in new)
        return kb - 1, live(runs), accs, runs

    _, _, accs, _ = lax.while_loop(cond, body, (i - 2, live(runs), tuple(accs), tuple(runs)))
    for h in range(hb):
        o_ref[:, h * SB_HEAD_DIM:(h + 1) * SB_HEAD_DIM] = accs[h].astype(o_ref.dtype)


def _attention(q, k, v, batch, seq, *, tq, heads_per_step=4):
    M, W = q.shape
    hw = heads_per_step * SB_HEAD_DIM
    nq = seq // tq
    tri = (jnp.arange(tq)[:, None] > jnp.arange(tq)[None, :]).astype(BF16)
    return pl.pallas_call(
        functools.partial(_attn_kernel, tq=tq),
        grid=(batch, W // hw, nq),
        in_specs=[pl.BlockSpec((tq, hw), lambda b, h, i: (b * nq + i, h)),
                  pl.BlockSpec((seq, hw), lambda b, h, i: (b, h)),
                  pl.BlockSpec((seq, hw), lambda b, h, i: (b, h)),
                  pl.BlockSpec((tq, tq), lambda b, h, i: (0, 0))],
        out_specs=pl.BlockSpec((tq, hw), lambda b, h, i: (b * nq + i, h)),
        out_shape=jax.ShapeDtypeStruct((M, W), BF16),
        compiler_params=_params(("arbitrary", "arbitrary", "arbitrary")),
        name="stickbreak_attn",
    )(q, k, v, tri)


def _tile(n, pref):
    t = pref
    while n % t:
        t //= 2
    return t


def _layer(x2d, batch, seq, norm1_w, w_in, conv_ssm_w, conv_ssm_b, dt_bias, a_log, d_skip, ssm_norm_w,
           q_norm_w, k_norm_w, gate_b, w_ssm_out, w_att_out, w_o, norm2_w, w_up, conv_ffn_w,
           conv_ffn_b, w_down):
    M, D = x2d.shape
    d_inner = w_ssm_out.shape[0]
    ssm_heads = dt_bias.shape[0]
    conv_dim = conv_ssm_w.shape[1]
    sb_width = w_att_out.shape[0]
    ffn = w_down.shape[0]
    tm = _tile(seq, 1024)
    tn = 1024

    o_dt = d_inner + conv_dim
    o_q = o_dt + ssm_heads
    assert d_inner % tn == 0 and conv_dim % tn == 0 and sb_width % tn == 0 and o_q % SUBLANES == 0
    w_in_t = jnp.swapaxes(w_in, 0, 1)
    w_dt = jnp.pad(w_in_t[o_dt:o_q], ((0, LANES - ssm_heads), (0, 0))).astype(BF16)
    pad_heads = lambda p: jnp.pad(p, (0, LANES - ssm_heads)).reshape(1, LANES)

    u = _rmsnorm(x2d, norm1_w, tm)

    tm_big = _tile(seq, 2048)

    def proj(name, w, start, n, epilogue, extras, dtype, tn=tn, tm=tm_big, scratch=()):
        return _mm(name, [u], [(w, start, True)], extras, epilogue, [((M, n), dtype, (tm, tn), _tile_map)],
                   tm=tm, tn=tn, n_cols=n, scratch=scratch)[0]

    zs = proj("proj_z", w_in_t, 0, d_inner, _ep_silu, [], BF16)
    xbc = proj("proj_xbc", w_in_t, d_inner, conv_dim,
               functools.partial(_ep_conv_silu, tiles_per_seq=seq // tm),
               [(conv_ssm_w, (SSM_CONV, tn), _col_map), (conv_ssm_b.reshape(1, -1), (1, tn), _col_map)],
               BF16, tm=tm, scratch=[pltpu.VMEM((SUBLANES, tn), F32)])
    dt = proj("proj_dt", w_dt, 0, LANES, _ep_softplus_bias,
              [(pad_heads(dt_bias), (1, LANES), _col_map)], F32, tn=LANES, tm=tm)
    scale = math.log2(math.e) / math.sqrt(SB_HEAD_DIM)
    head_w = lambda w: (w.reshape(1, SB_HEAD_DIM), (1, SB_HEAD_DIM), _fixed_map)
    q = proj("proj_q", w_in_t, o_q, sb_width, functools.partial(_ep_head_norm, scale=scale),
             [head_w(q_norm_w)], BF16)
    k = proj("proj_k", w_in_t, o_q + sb_width, sb_width, functools.partial(_ep_head_norm, scale=1.0),
             [head_w(k_norm_w)], BF16)
    v = proj("proj_v", w_in_t, o_q + 2 * sb_width, sb_width, _ep_store, [], BF16)
    gates = proj("proj_gate", w_in_t, o_q + 3 * sb_width, 2 * D, _ep_sigmoid_bias,
                 [(gate_b.reshape(1, -1), (1, tn), _col_map)], BF16)

    y_ssm_n = _ssd(xbc, dt, zs, pad_heads(a_log),
                   jnp.repeat(d_skip, SSM_HEAD_DIM).reshape(1, d_inner), ssm_norm_w, batch, seq)
    att = _attention(q, k, v, batch, seq, tq=_tile(seq, 256))

    tn_s = 512
    tm_s = _tile(seq, 512)
    nbm = D // tn_s
    mixed = _mm("mix", [y_ssm_n, att], [(w_ssm_out, 0, False), (w_att_out, 0, False)],
                [(gates, (tm_s, tn_s), _tile_map), (gates, (tm_s, tn_s), lambda j, i: (i, j + nbm))],
                _ep_mix, [((M, D), BF16, (tm_s, tn_s), _tile_map)], tm=tm_s, tn=tn_s, n_cols=D)[0]

    x1, u2 = _mm("out_proj", [mixed], [(w_o.astype(BF16), 0, False)],
                 [(x2d, (tm_s, D), _tile_map), (norm2_w.reshape(1, D), (1, D), _col_map)],
                 _ep_residual_norm,
                 [((M, D), F32, (tm_s, D), _tile_map), ((M, D), BF16, (tm_s, D), _tile_map)],
                 tm=tm_s, tn=D, n_cols=D)

    hdn = _mm("ffn_up", [u2, u2], [(w_up, 0, False), (w_up, ffn // tn_s, False)],
              [(conv_ffn_w, (FFN_CONV, tn_s), _col_map), (conv_ffn_b.reshape(1, -1), (1, tn_s), _col_map)],
              functools.partial(_ep_ffn_up, tiles_per_seq=seq // tm),
              [((M, ffn), BF16, (tm, tn_s), _tile_map)], tm=tm, tn=tn_s, n_cols=ffn,
              scratch=[pltpu.VMEM((SUBLANES, tn_s), F32)])[0]
    out = _mm("ffn_down", [hdn], [(w_down, 0, False)], [(x1, (tm_s, tn_s), _tile_map)], _ep_residual,
              [((M, D), F32, (tm_s, tn_s), _tile_map)], tm=tm_s, tn=tn_s, n_cols=D)[0]
    return out


def kernel(x, norm1_w, w_in, conv_ssm_w, conv_ssm_b, dt_bias, a_log, d_skip, ssm_norm_w, q_norm_w,
           k_norm_w, gate_b, w_ssm_out, w_att_out, w_o, norm2_w, w_up, conv_ffn_w, conv_ffn_b, w_down):
    b, s, d = x.shape
    x2d = x.reshape(b * s, d)
    for l in range(norm1_w.shape[0]):
        x2d = _layer(x2d, b, s, norm1_w[l], w_in[l], conv_ssm_w[l], conv_ssm_b[l], dt_bias[l], a_log[l],
                     d_skip[l], ssm_norm_w[l], q_norm_w[l], k_norm_w[l], gate_b[l], w_ssm_out[l],
                     w_att_out[l], w_o[l], norm2_w[l], w_up[l], conv_ffn_w[l], conv_ffn_b[l], w_down[l])
    return x2d.reshape(b, s, d)
```

```python
import functools
import math

import jax
import jax.numpy as jnp
from jax import lax
from jax.experimental import pallas as pl
from jax.experimental.pallas import tpu as pltpu

EPS = 1e-6
F32 = jnp.float32
BF16 = jnp.bfloat16

SSM_HEAD_DIM = 64
SSM_GROUPS = 8
SSM_STATE = 128
SSM_CONV = 4
CHUNK = 128
SB_HEAD_DIM = 128
FFN_CONV = 3

LANES = 128
SUBLANES = 8
VMEM_LIMIT_BYTES = 56 * 1024 * 1024

F32_EXP2_UNDERFLOW = 150.0


def _sigmoid(x):
    return 1.0 / (1.0 + jnp.exp(-x))


def _silu(x):
    return x * _sigmoid(x)


def _softplus(x):
    return jnp.maximum(x, 0.0) + jnp.log1p(jnp.exp(-jnp.abs(x)))


def _params(sem):
    return pltpu.CompilerParams(dimension_semantics=sem, vmem_limit_bytes=VMEM_LIMIT_BYTES)


def _rmsnorm_kernel(x_ref, w_ref, o_ref):
    x = x_ref[...]
    ms = jnp.mean(x * x, axis=-1, keepdims=True)
    o_ref[...] = (x * lax.rsqrt(ms + EPS) * w_ref[...]).astype(o_ref.dtype)


def _rmsnorm(x2d, w, tm):
    M, D = x2d.shape
    return pl.pallas_call(
        _rmsnorm_kernel,
        grid=(M // tm,),
        in_specs=[pl.BlockSpec((tm, D), lambda i: (i, 0)),
                  pl.BlockSpec((1, D), lambda i: (0, 0))],
        out_specs=pl.BlockSpec((tm, D), lambda i: (i, 0)),
        out_shape=jax.ShapeDtypeStruct((M, D), BF16),
        compiler_params=_params(("arbitrary",)),
        name="rmsnorm",
    )(x2d, w.reshape(1, D))


def _mm_kernel(*refs, n_mm, n_extra, n_out, stage, trans, rc, epilogue):
    a_refs = refs[:n_mm]
    w_refs = refs[n_mm:2 * n_mm]
    extra = refs[2 * n_mm:2 * n_mm + n_extra]
    outs = refs[2 * n_mm + n_extra:2 * n_mm + n_extra + n_out]
    scratch = list(refs[2 * n_mm + n_extra + n_out:])
    i = pl.program_id(1)
    w_bf = [scratch.pop(0) if s else w for w, s in zip(w_refs, stage)]

    @pl.when(i == 0)
    def _():
        for w, wb, s, t in zip(w_refs, w_bf, stage, trans):
            if s:
                wv = w[...].astype(BF16)
                wb[...] = wv.T if t else wv

    tm = a_refs[0].shape[0]
    carry = {}
    for r in range(tm // rc):
        rs = slice(r * rc, (r + 1) * rc)
        accs = [jnp.dot(a[rs, :], wb[...], preferred_element_type=F32) for a, wb in zip(a_refs, w_bf)]
        epilogue(accs, extra, outs, scratch, i, rs, carry)


def _mm(name, a_list, w_list, extras, epilogue, outs, *, tm, tn, n_cols, scratch=(), rc=128):
    M = a_list[0].shape[0]
    rc = min(rc, tm)
    in_specs = []
    for a in a_list:
        in_specs.append(pl.BlockSpec((tm, a.shape[1]), lambda j, i: (i, 0)))
    stage = []
    scratch_shapes = []
    for w, start, t in w_list:
        if t:
            k_dim = w.shape[1]
            in_specs.append(pl.BlockSpec((pl.Element(tn), pl.Element(k_dim)),
                                         functools.partial(
                                             lambda j, i, s: (pl.multiple_of(s + j * tn, SUBLANES), 0), s=start)))
        else:
            k_dim = w.shape[0]
            in_specs.append(pl.BlockSpec((k_dim, tn), functools.partial(lambda j, i, s: (0, j + s), s=start)))
        stage.append(t or w.dtype != BF16)
        if stage[-1]:
            scratch_shapes.append(pltpu.VMEM((k_dim, tn), BF16))
    for _, bs, im in extras:
        in_specs.append(pl.BlockSpec(bs, im))
    out_specs = [pl.BlockSpec(bs, im) for _, _, bs, im in outs]
    out_shape = [jax.ShapeDtypeStruct(s, d) for s, d, _, _ in outs]
    return pl.pallas_call(
        functools.partial(_mm_kernel, n_mm=len(a_list), n_extra=len(extras), n_out=len(outs),
                          stage=tuple(stage), trans=tuple(t for _, _, t in w_list), rc=rc, epilogue=epilogue),
        grid=(n_cols // tn, M // tm),
        in_specs=in_specs,
        out_specs=out_specs,
        out_shape=out_shape,
        scratch_shapes=scratch_shapes + list(scratch),
        compiler_params=_params(("arbitrary", "arbitrary")),
        name=name,
    )(*a_list, *[w[0] for w in w_list], *[e[0] for e in extras])


def _tile_map(j, i):
    return (i, j)


def _col_map(j, i):
    return (0, j)


def _fixed_map(j, i):
    return (0, 0)


def _ep_store(accs, extra, outs, scratch, i, rs, carry):
    outs[0][rs, :] = accs[0].astype(outs[0].dtype)


def _ep_silu(accs, extra, outs, scratch, i, rs, carry):
    outs[0][rs, :] = _silu(accs[0]).astype(outs[0].dtype)


def _ep_softplus_bias(accs, extra, outs, scratch, i, rs, carry):
    outs[0][rs, :] = _softplus(accs[0] + extra[0][...])


def _ep_sigmoid_bias(accs, extra, outs, scratch, i, rs, carry):
    outs[0][rs, :] = _sigmoid(accs[0] + extra[0][...]).astype(outs[0].dtype)


def _ep_head_norm(accs, extra, outs, scratch, i, rs, carry, *, scale):
    acc = accs[0]
    w = extra[0][...] * scale
    for c in range(acc.shape[1] // SB_HEAD_DIM):
        g = acc[:, c * SB_HEAD_DIM:(c + 1) * SB_HEAD_DIM]
        ms = jnp.mean(g * g, axis=-1, keepdims=True)
        outs[0][rs, c * SB_HEAD_DIM:(c + 1) * SB_HEAD_DIM] = (g * lax.rsqrt(ms + EPS) * w).astype(outs[0].dtype)


def _causal_conv(acc, cw_ref, cb_ref, tail_ref, i, rs, carry, tm, tiles_per_seq):
    taps = cw_ref.shape[0]
    if rs.start == 0:
        tail = jnp.where(i % tiles_per_seq == 0, 0.0, tail_ref[...])
    else:
        tail = carry["tail"]
    hist = jnp.concatenate([tail, acc], axis=0)
    y = cb_ref[...]
    for k in range(taps - 1):
        y = y + cw_ref[k:k + 1, :] * pltpu.roll(hist, taps - 1 - k, 0)[SUBLANES:, :]
    y = y + cw_ref[taps - 1:taps, :] * acc
    carry["tail"] = acc[acc.shape[0] - SUBLANES:, :]
    if rs.stop == tm:
        tail_ref[...] = carry["tail"]
    return y


def _ep_conv_silu(accs, extra, outs, scratch, i, rs, carry, *, tiles_per_seq):
    y = _causal_conv(accs[0], extra[0], extra[1], scratch[0], i, rs, carry, outs[0].shape[0], tiles_per_seq)
    outs[0][rs, :] = _silu(y).astype(outs[0].dtype)


def _ep_ffn_up(accs, extra, outs, scratch, i, rs, carry, *, tiles_per_seq):
    y = _causal_conv(accs[0], extra[0], extra[1], scratch[0], i, rs, carry, outs[0].shape[0], tiles_per_seq)
    outs[0][rs, :] = (_silu(y) * accs[1]).astype(outs[0].dtype)


def _ep_mix(accs, extra, outs, scratch, i, rs, carry):
    g0 = extra[0][rs, :].astype(F32)
    g1 = extra[1][rs, :].astype(F32)
    outs[0][rs, :] = (g0 * accs[0] + g1 * accs[1]).astype(outs[0].dtype)


def _ep_residual_norm(accs, extra, outs, scratch, i, rs, carry):
    x1 = extra[0][rs, :] + accs[0]
    outs[0][rs, :] = x1
    ms = jnp.mean(x1 * x1, axis=-1, keepdims=True)
    outs[1][rs, :] = (x1 * lax.rsqrt(ms + EPS) * extra[1][...]).astype(outs[1].dtype)


def _ep_residual(accs, extra, outs, scratch, i, rs, carry):
    outs[0][rs, :] = extra[0][rs, :] + accs[0]


def _split3(x):
    h = x.astype(BF16)
    r = x - h.astype(F32)
    m = r.astype(BF16)
    l = (r - m.astype(F32)).astype(BF16)
    return h, m, l


def _ssd_kernel(xbc_ref, dt_ref, zs_ref, alog_ref, dskip_ref, nw_ref, tri_ref,
                o_ref, st, ybuf, *, d_inner, n_groups):
    Q = CHUNK
    P = SSM_HEAD_DIM
    NS = SSM_STATE
    gw = d_inner // n_groups
    pairs = gw // (2 * P)
    c = pl.program_id(1)

    @pl.when(c == 0)
    def _():
        st[...] = jnp.zeros(st.shape, F32)

    dt = dt_ref[...]
    a = dt * (-jnp.exp(alog_ref[...]) * math.log2(math.e))
    tri = tri_ref[...]
    ah, am, al = _split3(a)
    acs = (jnp.dot(tri, ah, preferred_element_type=F32)
           + jnp.dot(tri, am, preferred_element_type=F32)
           + jnp.dot(tri, al, preferred_element_type=F32))
    src_t = (acs - jnp.log2(dt)).T
    wdec_t = (dt * jnp.exp2(acs[Q - 1:Q, :] - acs)).T
    qi = lax.broadcasted_iota(jnp.int32, (Q, Q), 0)
    si = lax.broadcasted_iota(jnp.int32, (Q, Q), 1)
    causal = qi >= si
    low = si < P
    low_b = jnp.where(low[0:1, :], 1.0, 0.0).astype(BF16)
    high_b = jnp.where(low[0:1, :], 0.0, 1.0).astype(BF16)

    for g in range(n_groups):
        b_g = xbc_ref[:, d_inner + g * NS:d_inner + (g + 1) * NS]
        c_g = xbc_ref[:, d_inner + (n_groups + g) * NS:d_inner + (n_groups + g + 1) * NS]
        cb = lax.dot_general(c_g, b_g, (((1,), (1,)), ((), ())), preferred_element_type=F32)
        b_t = b_g.astype(F32).T
        gl = slice(g * gw, (g + 1) * gw)
        y_off = jnp.dot(c_g, st[:, gl].astype(BF16), preferred_element_type=F32)
        for p in range(pairs):
            heads = (g * 2 * pairs + 2 * p, g * 2 * pairs + 2 * p + 1)
            pl_ = slice(g * gw + p * 2 * P, g * gw + (p + 1) * 2 * P)
            x_pair = xbc_ref[:, pl_]
            colb = [jnp.broadcast_to(acs[:, h:h + 1], (Q, Q)) for h in heads]
            ms = [(cb * jnp.exp2(jnp.where(causal, cbh - src_t[h:h + 1, :], -jnp.inf))).astype(BF16)
                  for h, cbh in zip(heads, colb)]
            bw = [(b_t * wdec_t[h:h + 1, :]).astype(BF16) for h in heads]
            lhs = jnp.concatenate([jnp.concatenate(ms, axis=1), jnp.concatenate(bw, axis=1)], axis=0)
            x_bd = jnp.concatenate([x_pair * low_b, x_pair * high_b], axis=0)
            res = jnp.dot(lhs, x_bd, preferred_element_type=F32)
            e_pair = jnp.exp2(jnp.where(low, colb[0], colb[1]))
            ybuf[:, pl_] = (res[:Q] + e_pair * y_off[:, p * 2 * P:(p + 1) * 2 * P]
                            + dskip_ref[:, pl_] * x_pair.astype(F32))
            st[:, pl_] = st[:, pl_] * e_pair[Q - 1:Q, :] + res[Q:]

    for g in range(n_groups):
        gl = slice(g * gw, (g + 1) * gw)
        yg = ybuf[:, gl] * zs_ref[:, gl].astype(F32)
        ms = jnp.mean(yg * yg, axis=-1, keepdims=True)
        o_ref[:, gl] = (yg * lax.rsqrt(ms + EPS) * nw_ref[:, gl]).astype(o_ref.dtype)


def _ssd(xbc, dt, zs, a_log_pad, dskip_row, norm_w, batch, seq):
    M, conv_dim = xbc.shape
    d_inner = zs.shape[1]
    n_groups = SSM_GROUPS
    nc = seq // CHUNK
    tri = (jnp.arange(CHUNK)[:, None] >= jnp.arange(CHUNK)[None, :]).astype(BF16)
    row = lambda b, c: (b * nc + c, 0)
    fixed = lambda b, c: (0, 0)
    return pl.pallas_call(
        functools.partial(_ssd_kernel, d_inner=d_inner, n_groups=n_groups),
        grid=(batch, nc),
        in_specs=[pl.BlockSpec((CHUNK, conv_dim), row),
                  pl.BlockSpec((CHUNK, LANES), row),
                  pl.BlockSpec((CHUNK, d_inner), row),
                  pl.BlockSpec((1, LANES), fixed),
                  pl.BlockSpec((1, d_inner), fixed),
                  pl.BlockSpec((1, d_inner), fixed),
                  pl.BlockSpec((CHUNK, CHUNK), fixed)],
        out_specs=pl.BlockSpec((CHUNK, d_inner), row),
        out_shape=jax.ShapeDtypeStruct((M, d_inner), BF16),
        scratch_shapes=[pltpu.VMEM((SSM_STATE, d_inner), F32),
                        pltpu.VMEM((CHUNK, d_inner), F32)],
        compiler_params=_params(("arbitrary", "arbitrary")),
        name="ssd",
    )(xbc, dt, zs, a_log_pad, dskip_row, norm_w.reshape(1, d_inner), tri)


def _attn_kernel(q_ref, k_ref, v_ref, tri_ref, o_ref, *, tq):
    i = pl.program_id(2)
    hb = q_ref.shape[1] // SB_HEAD_DIM
    tri = tri_ref[...]
    rows = lax.broadcasted_iota(jnp.int32, (tq, tq), 0)
    cols = lax.broadcasted_iota(jnp.int32, (tq, tq), 1)
    mask = cols < rows

    def block(h, kb, run, diagonal):
        ks = pl.multiple_of(kb * tq, tq)
        hs = slice(h * SB_HEAD_DIM, (h + 1) * SB_HEAD_DIM)
        kblk = k_ref[pl.ds(ks, tq), hs]
        vblk = v_ref[pl.ds(ks, tq), hs]
        z = lax.dot_general(q_ref[:, hs], kblk, (((1,), (1,)), ((), ())), preferred_element_type=F32)
        sp = jnp.maximum(z, 0.0) + jnp.log2(1.0 + jnp.exp2(-jnp.abs(z)))
        nlk = jnp.where(mask, sp, 0.0) if diagonal else sp
        later = jnp.dot(nlk.astype(BF16), tri, preferred_element_type=F32)
        att = jnp.exp2(((z - sp) - later) - run)
        if diagonal:
            att = jnp.where(mask, att, 0.0)
        pv = jnp.dot(att.astype(BF16), vblk, preferred_element_type=F32)
        return pv, run + jnp.sum(nlk, axis=-1, keepdims=True)

    def live(runs):
        return jnp.min(functools.reduce(jnp.minimum, runs)) < F32_EXP2_UNDERFLOW

    has_prev = i > 0
    accs, runs = [], []
    for h in range(hb):
        acc, run = block(h, i, jnp.zeros((tq, 1), F32), True)
        pv, run2 = block(h, jnp.maximum(i - 1, 0), run, False)
        accs.append(acc + jnp.where(has_prev, pv, 0.0))
        runs.append(jnp.where(has_prev, run2, run))

    def cond(c):
        return jnp.logical_and(c[0] >= 0, c[1])

    def body(c):
        kb, _, accs, runs = c
        new = [block(h, kb, runs[h], False) for h in range(hb)]
        accs = tuple(a + pv for a, (pv, _) in zip(accs, new))
        runs = tuple(r for _, r in new)
        return kb - 1, live(runs), accs, runs

    _, _, accs, _ = lax.while_loop(cond, body, (i - 2, live(runs), tuple(accs), tuple(runs)))
    for h in range(hb):
        o_ref[:, h * SB_HEAD_DIM:(h + 1) * SB_HEAD_DIM] = accs[h].astype(o_ref.dtype)


def _attention(q, k, v, batch, seq, *, tq, heads_per_step=4):
    M, W = q.shape
    hw = heads_per_step * SB_HEAD_DIM
    nq = seq // tq
    tri = (jnp.arange(tq)[:, None] > jnp.arange(tq)[None, :]).astype(BF16)
    return pl.pallas_call(
        functools.partial(_attn_kernel, tq=tq),
        grid=(batch, W // hw, nq),
        in_specs=[pl.BlockSpec((tq, hw), lambda b, h, i: (b * nq + i, h)),
                  pl.BlockSpec((seq, hw), lambda b, h, i: (b, h)),
                  pl.BlockSpec((seq, hw), lambda b, h, i: (b, h)),
                  pl.BlockSpec((tq, tq), lambda b, h, i: (0, 0))],
        out_specs=pl.BlockSpec((tq, hw), lambda b, h, i: (b * nq + i, h)),
        out_shape=jax.ShapeDtypeStruct((M, W), BF16),
        compiler_params=_params(("arbitrary", "arbitrary", "arbitrary")),
        name="stickbreak_attn",
    )(q, k, v, tri)


def _tile(n, pref):
    t = pref
    while n % t:
        t //= 2
    return t


def _layer(x2d, batch, seq, norm1_w, w_in, conv_ssm_w, conv_ssm_b, dt_bias, a_log, d_skip, ssm_norm_w,
           q_norm_w, k_norm_w, gate_b, w_ssm_out, w_att_out, w_o, norm2_w, w_up, conv_ffn_w,
           conv_ffn_b, w_down):
    M, D = x2d.shape
    d_inner = w_ssm_out.shape[0]
    ssm_heads = dt_bias.shape[0]
    conv_dim = conv_ssm_w.shape[1]
    sb_width = w_att_out.shape[0]
    ffn = w_down.shape[0]
    tm = _tile(seq, 1024)
    tn = 1024

    o_dt = d_inner + conv_dim
    o_q = o_dt + ssm_heads
    assert d_inner % tn == 0 and conv_dim % tn == 0 and sb_width % tn == 0 and o_q % SUBLANES == 0
    w_in_t = jnp.swapaxes(w_in, 0, 1)
    w_dt = jnp.pad(w_in_t[o_dt:o_q], ((0, LANES - ssm_heads), (0, 0))).astype(BF16)
    pad_heads = lambda p: jnp.pad(p, (0, LANES - ssm_heads)).reshape(1, LANES)

    u = _rmsnorm(x2d, norm1_w, tm)

    tm_big = _tile(seq, 2048)

    def proj(name, w, start, n, epilogue, extras, dtype, tn=tn, tm=tm_big, scratch=()):
        return _mm(name, [u], [(w, start, True)], extras, epilogue, [((M, n), dtype, (tm, tn), _tile_map)],
                   tm=tm, tn=tn, n_cols=n, scratch=scratch)[0]

    zs = proj("proj_z", w_in_t, 0, d_inner, _ep_silu, [], BF16)
    tn_c = 512
    xbc = proj("proj_xbc", w_in_t, d_inner, conv_dim,
               functools.partial(_ep_conv_silu, tiles_per_seq=seq // tm),
               [(conv_ssm_w, (SSM_CONV, tn_c), _col_map), (conv_ssm_b.reshape(1, -1), (1, tn_c), _col_map)],
               BF16, tn=tn_c, tm=tm, scratch=[pltpu.VMEM((SUBLANES, tn_c), F32)])
    dt = proj("proj_dt", w_dt, 0, LANES, _ep_softplus_bias,
              [(pad_heads(dt_bias), (1, LANES), _col_map)], F32, tn=LANES, tm=tm)
    scale = math.log2(math.e) / math.sqrt(SB_HEAD_DIM)
    head_w = lambda w: (w.reshape(1, SB_HEAD_DIM), (1, SB_HEAD_DIM), _fixed_map)
    q = proj("proj_q", w_in_t, o_q, sb_width, functools.partial(_ep_head_norm, scale=scale),
             [head_w(q_norm_w)], BF16)
    k = proj("proj_k", w_in_t, o_q + sb_width, sb_width, functools.partial(_ep_head_norm, scale=1.0),
             [head_w(k_norm_w)], BF16)
    v = proj("proj_v", w_in_t, o_q + 2 * sb_width, sb_width, _ep_store, [], BF16)
    gates = proj("proj_gate", w_in_t, o_q + 3 * sb_width, 2 * D, _ep_sigmoid_bias,
                 [(gate_b.reshape(1, -1), (1, tn), _col_map)], BF16)

    y_ssm_n = _ssd(xbc, dt, zs, pad_heads(a_log),
                   jnp.repeat(d_skip, SSM_HEAD_DIM).reshape(1, d_inner), ssm_norm_w, batch, seq)
    att = _attention(q, k, v, batch, seq, tq=_tile(seq, 256))

    tn_s = 512
    tm_s = _tile(seq, 512)
    nbm = D // tn_s
    mixed = _mm("mix", [y_ssm_n, att], [(w_ssm_out, 0, False), (w_att_out, 0, False)],
                [(gates, (tm_s, tn_s), _tile_map), (gates, (tm_s, tn_s), lambda j, i: (i, j + nbm))],
                _ep_mix, [((M, D), BF16, (tm_s, tn_s), _tile_map)], tm=tm_s, tn=tn_s, n_cols=D)[0]

    x1, u2 = _mm("out_proj", [mixed], [(w_o.astype(BF16), 0, False)],
                 [(x2d, (tm_s, D), _tile_map), (norm2_w.reshape(1, D), (1, D), _col_map)],
                 _ep_residual_norm,
                 [((M, D), F32, (tm_s, D), _tile_map), ((M, D), BF16, (tm_s, D), _tile_map)],
                 tm=tm_s, tn=D, n_cols=D)

    hdn = _mm("ffn_up", [u2, u2], [(w_up, 0, False), (w_up, ffn // tn_s, False)],
              [(conv_ffn_w, (FFN_CONV, tn_s), _col_map), (conv_ffn_b.reshape(1, -1), (1, tn_s), _col_map)],
              functools.partial(_ep_ffn_up, tiles_per_seq=seq // tm),
              [((M, ffn), BF16, (tm, tn_s), _tile_map)], tm=tm, tn=tn_s, n_cols=ffn,
              scratch=[pltpu.VMEM((SUBLANES, tn_s), F32)])[0]
    out = _mm("ffn_down", [hdn], [(w_down, 0, False)], [(x1, (tm_s, tn_s), _tile_map)], _ep_residual,
              [((M, D), F32, (tm_s, tn_s), _tile_map)], tm=tm_s, tn=tn_s, n_cols=D)[0]
    return out


def kernel(x, norm1_w, w_in, conv_ssm_w, conv_ssm_b, dt_bias, a_log, d_skip, ssm_norm_w, q_norm_w,
           k_norm_w, gate_b, w_ssm_out, w_att_out, w_o, norm2_w, w_up, conv_ffn_w, conv_ffn_b, w_down):
    b, s, d = x.shape
    x2d = x.reshape(b * s, d)
    for l in range(norm1_w.shape[0]):
        x2d = _layer(x2d, b, s, norm1_w[l], w_in[l], conv_ssm_w[l], conv_ssm_b[l], dt_bias[l], a_log[l],
                     d_skip[l], ssm_norm_w[l], q_norm_w[l], k_norm_w[l], gate_b[l], w_ssm_out[l],
                     w_att_out[l], w_o[l], norm2_w[l], w_up[l], conv_ffn_w[l], conv_ffn_b[l], w_down[l])
    return x2d.reshape(b, s, d)
```

```python
import functools
import math

import jax
import jax.numpy as jnp
from jax import lax
from jax.experimental import pallas as pl
from jax.experimental.pallas import tpu as pltpu

EPS = 1e-6
F32 = jnp.float32
BF16 = jnp.bfloat16

SSM_HEAD_DIM = 64
SSM_GROUPS = 8
SSM_STATE = 128
SSM_CONV = 4
CHUNK = 128
SB_HEAD_DIM = 128
FFN_CONV = 3

LANES = 128
SUBLANES = 8
VMEM_LIMIT_BYTES = 56 * 1024 * 1024

F32_EXP2_UNDERFLOW = 150.0


def _sigmoid(x):
    return 1.0 / (1.0 + jnp.exp(-x))


def _silu(x):
    return x * _sigmoid(x)


def _softplus(x):
    return jnp.maximum(x, 0.0) + jnp.log1p(jnp.exp(-jnp.abs(x)))


def _params(sem):
    return pltpu.CompilerParams(dimension_semantics=sem, vmem_limit_bytes=VMEM_LIMIT_BYTES)


def _rmsnorm_kernel(x_ref, w_ref, o_ref):
    x = x_ref[...]
    ms = jnp.mean(x * x, axis=-1, keepdims=True)
    o_ref[...] = (x * lax.rsqrt(ms + EPS) * w_ref[...]).astype(o_ref.dtype)


def _rmsnorm(x2d, w, tm):
    M, D = x2d.shape
    return pl.pallas_call(
        _rmsnorm_kernel,
        grid=(M // tm,),
        in_specs=[pl.BlockSpec((tm, D), lambda i: (i, 0)),
                  pl.BlockSpec((1, D), lambda i: (0, 0))],
        out_specs=pl.BlockSpec((tm, D), lambda i: (i, 0)),
        out_shape=jax.ShapeDtypeStruct((M, D), BF16),
        compiler_params=_params(("arbitrary",)),
        name="rmsnorm",
    )(x2d, w.reshape(1, D))


def _mm_kernel(*refs, n_a, n_mm, n_extra, n_out, stage, trans, rc, epilogue):
    a_refs = [refs[k % n_a] for k in range(n_mm)]
    refs = refs[n_a:]
    w_refs = refs[:n_mm]
    extra = refs[n_mm:n_mm + n_extra]
    outs = refs[n_mm + n_extra:n_mm + n_extra + n_out]
    scratch = list(refs[n_mm + n_extra + n_out:])
    i = pl.program_id(1)
    w_bf = [scratch.pop(0) if s else w for w, s in zip(w_refs, stage)]

    @pl.when(i == 0)
    def _():
        for w, wb, s, t in zip(w_refs, w_bf, stage, trans):
            if s:
                wv = w[...].astype(BF16)
                wb[...] = wv.T if t else wv

    tm = a_refs[0].shape[0]
    carry = {}
    def products(r):
        rs = slice(r * rc, (r + 1) * rc)
        return [jnp.dot(a[rs, :], wb[...], preferred_element_type=F32) for a, wb in zip(a_refs, w_bf)]

    nxt = products(0)
    for r in range(tm // rc):
        accs = nxt
        if r + 1 < tm // rc:
            nxt = products(r + 1)
        epilogue(accs, extra, outs, scratch, i, slice(r * rc, (r + 1) * rc), carry)


def _mm(name, a_list, w_list, extras, epilogue, outs, *, tm, tn, n_cols, scratch=(), rc=128):
    M = a_list[0].shape[0]
    rc = min(rc, tm)
    in_specs = []
    for a in a_list:
        in_specs.append(pl.BlockSpec((tm, a.shape[1]), lambda j, i: (i, 0)))
    stage = []
    scratch_shapes = []
    for w, start, t in w_list:
        if t:
            k_dim = w.shape[1]
            in_specs.append(pl.BlockSpec((pl.Element(tn), pl.Element(k_dim)),
                                         functools.partial(
                                             lambda j, i, s: (pl.multiple_of(s + j * tn, SUBLANES), 0), s=start)))
        else:
            k_dim = w.shape[0]
            in_specs.append(pl.BlockSpec((k_dim, tn), functools.partial(lambda j, i, s: (0, j + s), s=start)))
        stage.append(t or w.dtype != BF16)
        if stage[-1]:
            scratch_shapes.append(pltpu.VMEM((k_dim, tn), BF16))
    for _, bs, im in extras:
        in_specs.append(pl.BlockSpec(bs, im))
    out_specs = [pl.BlockSpec(bs, im) for _, _, bs, im in outs]
    out_shape = [jax.ShapeDtypeStruct(s, d) for s, d, _, _ in outs]
    return pl.pallas_call(
        functools.partial(_mm_kernel, n_a=len(a_list), n_mm=len(w_list), n_extra=len(extras), n_out=len(outs),
                          stage=tuple(stage), trans=tuple(t for _, _, t in w_list), rc=rc, epilogue=epilogue),
        grid=(n_cols // tn, M // tm),
        in_specs=in_specs,
        out_specs=out_specs,
        out_shape=out_shape,
        scratch_shapes=scratch_shapes + list(scratch),
        compiler_params=_params(("arbitrary", "arbitrary")),
        name=name,
    )(*a_list, *[w[0] for w in w_list], *[e[0] for e in extras])


def _tile_map(j, i):
    return (i, j)


def _col_map(j, i):
    return (0, j)


def _fixed_map(j, i):
    return (0, 0)


def _ep_store(accs, extra, outs, scratch, i, rs, carry):
    outs[0][rs, :] = accs[0].astype(outs[0].dtype)


def _ep_silu(accs, extra, outs, scratch, i, rs, carry):
    outs[0][rs, :] = _silu(accs[0]).astype(outs[0].dtype)


def _ep_softplus_bias(accs, extra, outs, scratch, i, rs, carry):
    outs[0][rs, :] = _softplus(accs[0] + extra[0][...])


def _ep_sigmoid_bias(accs, extra, outs, scratch, i, rs, carry):
    outs[0][rs, :] = _sigmoid(accs[0] + extra[0][...]).astype(outs[0].dtype)


def _ep_head_norm(accs, extra, outs, scratch, i, rs, carry, *, scale):
    acc = accs[0]
    w = extra[0][...] * scale
    for c in range(acc.shape[1] // SB_HEAD_DIM):
        g = acc[:, c * SB_HEAD_DIM:(c + 1) * SB_HEAD_DIM]
        ms = jnp.mean(g * g, axis=-1, keepdims=True)
        outs[0][rs, c * SB_HEAD_DIM:(c + 1) * SB_HEAD_DIM] = (g * lax.rsqrt(ms + EPS) * w).astype(outs[0].dtype)


def _causal_conv(acc, cw_ref, cb_ref, tail_ref, i, rs, carry, tm, tiles_per_seq):
    taps = cw_ref.shape[0]
    if rs.start == 0:
        tail = jnp.where(i % tiles_per_seq == 0, 0.0, tail_ref[...])
    else:
        tail = carry["tail"]
    hist = jnp.concatenate([tail, acc], axis=0)
    y = cb_ref[...]
    for k in range(taps - 1):
        y = y + cw_ref[k:k + 1, :] * pltpu.roll(hist, taps - 1 - k, 0)[SUBLANES:, :]
    y = y + cw_ref[taps - 1:taps, :] * acc
    carry["tail"] = acc[acc.shape[0] - SUBLANES:, :]
    if rs.stop == tm:
        tail_ref[...] = carry["tail"]
    return y


def _ep_conv_silu(accs, extra, outs, scratch, i, rs, carry, *, tiles_per_seq):
    y = _causal_conv(accs[0], extra[0], extra[1], scratch[0], i, rs, carry, outs[0].shape[0], tiles_per_seq)
    outs[0][rs, :] = _silu(y).astype(outs[0].dtype)


def _ep_ffn_up(accs, extra, outs, scratch, i, rs, carry, *, tiles_per_seq):
    y = _causal_conv(accs[0], extra[0], extra[1], scratch[0], i, rs, carry, outs[0].shape[0], tiles_per_seq)
    outs[0][rs, :] = (_silu(y) * accs[1]).astype(outs[0].dtype)


def _ep_mix(accs, extra, outs, scratch, i, rs, carry):
    g0 = extra[0][rs, :].astype(F32)
    g1 = extra[1][rs, :].astype(F32)
    outs[0][rs, :] = (g0 * accs[0] + g1 * accs[1]).astype(outs[0].dtype)


def _ep_residual_norm(accs, extra, outs, scratch, i, rs, carry):
    x1 = extra[0][rs, :] + accs[0]
    outs[0][rs, :] = x1
    ms = jnp.mean(x1 * x1, axis=-1, keepdims=True)
    outs[1][rs, :] = (x1 * lax.rsqrt(ms + EPS) * extra[1][...]).astype(outs[1].dtype)


def _ep_residual(accs, extra, outs, scratch, i, rs, carry):
    outs[0][rs, :] = extra[0][rs, :] + accs[0]


def _split3(x):
    h = x.astype(BF16)
    r = x - h.astype(F32)
    m = r.astype(BF16)
    l = (r - m.astype(F32)).astype(BF16)
    return h, m, l


def _ssd_kernel(xbc_ref, dt_ref, zs_ref, alog_ref, dskip_ref, nw_ref, tri_ref,
                o_ref, st, ybuf, *, d_inner, n_groups):
    Q = CHUNK
    P = SSM_HEAD_DIM
    NS = SSM_STATE
    gw = d_inner // n_groups
    pairs = gw // (2 * P)
    c = pl.program_id(1)

    @pl.when(c == 0)
    def _():
        st[...] = jnp.zeros(st.shape, F32)

    dt = dt_ref[...]
    a = dt * (-jnp.exp(alog_ref[...]) * math.log2(math.e))
    tri = tri_ref[...]
    ah, am, al = _split3(a)
    acs = (jnp.dot(tri, ah, preferred_element_type=F32)
           + jnp.dot(tri, am, preferred_element_type=F32)
           + jnp.dot(tri, al, preferred_element_type=F32))
    src_t = (acs - jnp.log2(dt)).T
    wdec_t = (dt * jnp.exp2(acs[Q - 1:Q, :] - acs)).T
    qi = lax.broadcasted_iota(jnp.int32, (Q, Q), 0)
    si = lax.broadcasted_iota(jnp.int32, (Q, Q), 1)
    causal = qi >= si
    low = si < P
    low_b = jnp.where(low[0:1, :], 1.0, 0.0).astype(BF16)
    high_b = jnp.where(low[0:1, :], 0.0, 1.0).astype(BF16)

    for g in range(n_groups):
        b_g = xbc_ref[:, d_inner + g * NS:d_inner + (g + 1) * NS]
        c_g = xbc_ref[:, d_inner + (n_groups + g) * NS:d_inner + (n_groups + g + 1) * NS]
        cb = lax.dot_general(c_g, b_g, (((1,), (1,)), ((), ())), preferred_element_type=F32)
        b_t = b_g.astype(F32).T
        gl = slice(g * gw, (g + 1) * gw)
        y_off = jnp.dot(c_g, st[:, gl].astype(BF16), preferred_element_type=F32)
        for p in range(pairs):
            heads = (g * 2 * pairs + 2 * p, g * 2 * pairs + 2 * p + 1)
            pl_ = slice(g * gw + p * 2 * P, g * gw + (p + 1) * 2 * P)
            x_pair = xbc_ref[:, pl_]
            colb = [jnp.broadcast_to(acs[:, h:h + 1], (Q, Q)) for h in heads]
            ms = [(cb * jnp.exp2(jnp.where(causal, cbh - src_t[h:h + 1, :], -jnp.inf))).astype(BF16)
                  for h, cbh in zip(heads, colb)]
            bw = [(b_t * wdec_t[h:h + 1, :]).astype(BF16) for h in heads]
            lhs = jnp.concatenate([jnp.concatenate(ms, axis=1), jnp.concatenate(bw, axis=1)], axis=0)
            x_bd = jnp.concatenate([x_pair * low_b, x_pair * high_b], axis=0)
            res = jnp.dot(lhs, x_bd, preferred_element_type=F32)
            e_pair = jnp.exp2(jnp.where(low, colb[0], colb[1]))
            ybuf[:, pl_] = (res[:Q] + e_pair * y_off[:, p * 2 * P:(p + 1) * 2 * P]
                            + dskip_ref[:, pl_] * x_pair.astype(F32))
            st[:, pl_] = st[:, pl_] * e_pair[Q - 1:Q, :] + res[Q:]

    for g in range(n_groups):
        gl = slice(g * gw, (g + 1) * gw)
        yg = ybuf[:, gl] * zs_ref[:, gl].astype(F32)
        ms = jnp.mean(yg * yg, axis=-1, keepdims=True)
        o_ref[:, gl] = (yg * lax.rsqrt(ms + EPS) * nw_ref[:, gl]).astype(o_ref.dtype)


def _ssd(xbc, dt, zs, a_log_pad, dskip_row, norm_w, batch, seq):
    M, conv_dim = xbc.shape
    d_inner = zs.shape[1]
    n_groups = SSM_GROUPS
    nc = seq // CHUNK
    tri = (jnp.arange(CHUNK)[:, None] >= jnp.arange(CHUNK)[None, :]).astype(BF16)
    row = lambda b, c: (b * nc + c, 0)
    fixed = lambda b, c: (0, 0)
    return pl.pallas_call(
        functools.partial(_ssd_kernel, d_inner=d_inner, n_groups=n_groups),
        grid=(batch, nc),
        in_specs=[pl.BlockSpec((CHUNK, conv_dim), row),
                  pl.BlockSpec((CHUNK, LANES), row),
                  pl.BlockSpec((CHUNK, d_inner), row),
                  pl.BlockSpec((1, LANES), fixed),
                  pl.BlockSpec((1, d_inner), fixed),
                  pl.BlockSpec((1, d_inner), fixed),
                  pl.BlockSpec((CHUNK, CHUNK), fixed)],
        out_specs=pl.BlockSpec((CHUNK, d_inner), row),
        out_shape=jax.ShapeDtypeStruct((M, d_inner), BF16),
        scratch_shapes=[pltpu.VMEM((SSM_STATE, d_inner), F32),
                        pltpu.VMEM((CHUNK, d_inner), F32)],
        compiler_params=_params(("arbitrary", "arbitrary")),
        name="ssd",
    )(xbc, dt, zs, a_log_pad, dskip_row, norm_w.reshape(1, d_inner), tri)


def _attn_kernel(q_ref, k_ref, v_ref, tri_ref, o_ref, *, tq):
    i = pl.program_id(2)
    hb = q_ref.shape[1] // SB_HEAD_DIM
    tri = tri_ref[...]
    rows = lax.broadcasted_iota(jnp.int32, (tq, tq), 0)
    cols = lax.broadcasted_iota(jnp.int32, (tq, tq), 1)
    mask = cols < rows

    def block(h, kb, run, diagonal):
        ks = pl.multiple_of(kb * tq, tq)
        hs = slice(h * SB_HEAD_DIM, (h + 1) * SB_HEAD_DIM)
        kblk = k_ref[pl.ds(ks, tq), hs]
        vblk = v_ref[pl.ds(ks, tq), hs]
        z = lax.dot_general(q_ref[:, hs], kblk, (((1,), (1,)), ((), ())), preferred_element_type=F32)
        sp = jnp.maximum(z, 0.0) + jnp.log2(1.0 + jnp.exp2(-jnp.abs(z)))
        nlk = jnp.where(mask, sp, 0.0) if diagonal else sp
        later = jnp.dot(nlk.astype(BF16), tri, preferred_element_type=F32)
        att = jnp.exp2(((z - sp) - later) - run)
        if diagonal:
            att = jnp.where(mask, att, 0.0)
        pv = jnp.dot(att.astype(BF16), vblk, preferred_element_type=F32)
        return pv, run + jnp.sum(nlk, axis=-1, keepdims=True)

    def live(runs):
        return jnp.min(functools.reduce(jnp.minimum, runs)) < F32_EXP2_UNDERFLOW

    has_prev = i > 0
    accs, runs = [], []
    for h in range(hb):
        acc, run = block(h, i, jnp.zeros((tq, 1), F32), True)
        pv, run2 = block(h, jnp.maximum(i - 1, 0), run, False)
        accs.append(acc + jnp.where(has_prev, pv, 0.0))
        runs.append(jnp.where(has_prev, run2, run))

    def cond(c):
        return jnp.logical_and(c[0] >= 0, c[1])

    def body(c):
        kb, _, accs, runs = c
        new = [block(h, kb, runs[h], False) for h in range(hb)]
        accs = tuple(a + pv for a, (pv, _) in zip(accs, new))
        runs = tuple(r for _, r in new)
        return kb - 1, live(runs), accs, runs

    _, _, accs, _ = lax.while_loop(cond, body, (i - 2, live(runs), tuple(accs), tuple(runs)))
    for h in range(hb):
        o_ref[:, h * SB_HEAD_DIM:(h + 1) * SB_HEAD_DIM] = accs[h].astype(o_ref.dtype)


def _attention(q, k, v, batch, seq, *, tq, heads_per_step=4):
    M, W = q.shape
    hw = heads_per_step * SB_HEAD_DIM
    nq = seq // tq
    tri = (jnp.arange(tq)[:, None] > jnp.arange(tq)[None, :]).astype(BF16)
    return pl.pallas_call(
        functools.partial(_attn_kernel, tq=tq),
        grid=(batch, W // hw, nq),
        in_specs=[pl.BlockSpec((tq, hw), lambda b, h, i: (b * nq + i, h)),
                  pl.BlockSpec((seq, hw), lambda b, h, i: (b, h)),
                  pl.BlockSpec((seq, hw), lambda b, h, i: (b, h)),
                  pl.BlockSpec((tq, tq), lambda b, h, i: (0, 0))],
        out_specs=pl.BlockSpec((tq, hw), lambda b, h, i: (b * nq + i, h)),
        out_shape=jax.ShapeDtypeStruct((M, W), BF16),
        compiler_params=_params(("arbitrary", "arbitrary", "arbitrary")),
        name="stickbreak_attn",
    )(q, k, v, tri)


def _tile(n, pref):
    t = pref
    while n % t:
        t //= 2
    return t


def _layer(x2d, batch, seq, norm1_w, w_in, conv_ssm_w, conv_ssm_b, dt_bias, a_log, d_skip, ssm_norm_w,
           q_norm_w, k_norm_w, gate_b, w_ssm_out, w_att_out, w_o, norm2_w, w_up, conv_ffn_w,
           conv_ffn_b, w_down):
    M, D = x2d.shape
    d_inner = w_ssm_out.shape[0]
    ssm_heads = dt_bias.shape[0]
    conv_dim = conv_ssm_w.shape[1]
    sb_width = w_att_out.shape[0]
    ffn = w_down.shape[0]
    tm = _tile(seq, 1024)
    tn = 1024

    o_dt = d_inner + conv_dim
    o_q = o_dt + ssm_heads
    assert d_inner % tn == 0 and conv_dim % tn == 0 and sb_width % tn == 0 and o_q % SUBLANES == 0
    w_in_t = jnp.swapaxes(w_in, 0, 1)
    w_dt = jnp.pad(w_in_t[o_dt:o_q], ((0, LANES - ssm_heads), (0, 0))).astype(BF16)
    pad_heads = lambda p: jnp.pad(p, (0, LANES - ssm_heads)).reshape(1, LANES)

    u = _rmsnorm(x2d, norm1_w, tm)

    tm_big = _tile(seq, 2048)

    def proj(name, w, start, n, epilogue, extras, dtype, tn=tn, tm=tm_big, scratch=()):
        return _mm(name, [u], [(w, start, True)], extras, epilogue, [((M, n), dtype, (tm, tn), _tile_map)],
                   tm=tm, tn=tn, n_cols=n, scratch=scratch)[0]

    zs = proj("proj_z", w_in_t, 0, d_inner, _ep_silu, [], BF16)
    tn_c = 512
    xbc = proj("proj_xbc", w_in_t, d_inner, conv_dim,
               functools.partial(_ep_conv_silu, tiles_per_seq=seq // tm_big),
               [(conv_ssm_w, (SSM_CONV, tn_c), _col_map), (conv_ssm_b.reshape(1, -1), (1, tn_c), _col_map)],
               BF16, tn=tn_c, scratch=[pltpu.VMEM((SUBLANES, tn_c), F32)])
    dt = proj("proj_dt", w_dt, 0, LANES, _ep_softplus_bias,
              [(pad_heads(dt_bias), (1, LANES), _col_map)], F32, tn=LANES, tm=tm)
    scale = math.log2(math.e) / math.sqrt(SB_HEAD_DIM)
    head_w = lambda w: (w.reshape(1, SB_HEAD_DIM), (1, SB_HEAD_DIM), _fixed_map)
    q = proj("proj_q", w_in_t, o_q, sb_width, functools.partial(_ep_head_norm, scale=scale),
             [head_w(q_norm_w)], BF16)
    k = proj("proj_k", w_in_t, o_q + sb_width, sb_width, functools.partial(_ep_head_norm, scale=1.0),
             [head_w(k_norm_w)], BF16)
    v = proj("proj_v", w_in_t, o_q + 2 * sb_width, sb_width, _ep_store, [], BF16)
    gates = proj("proj_gate", w_in_t, o_q + 3 * sb_width, 2 * D, _ep_sigmoid_bias,
                 [(gate_b.reshape(1, -1), (1, tn), _col_map)], BF16)

    y_ssm_n = _ssd(xbc, dt, zs, pad_heads(a_log),
                   jnp.repeat(d_skip, SSM_HEAD_DIM).reshape(1, d_inner), ssm_norm_w, batch, seq)
    att = _attention(q, k, v, batch, seq, tq=_tile(seq, 256))

    tn_s = 512
    tm_s = _tile(seq, 512)
    nbm = D // tn_s
    mixed = _mm("mix", [y_ssm_n, att], [(w_ssm_out, 0, False), (w_att_out, 0, False)],
                [(gates, (tm_s, tn_s), _tile_map), (gates, (tm_s, tn_s), lambda j, i: (i, j + nbm))],
                _ep_mix, [((M, D), BF16, (tm_s, tn_s), _tile_map)], tm=tm_s, tn=tn_s, n_cols=D)[0]

    x1, u2 = _mm("out_proj", [mixed], [(w_o.astype(BF16), 0, False)],
                 [(x2d, (tm_s, D), _tile_map), (norm2_w.reshape(1, D), (1, D), _col_map)],
                 _ep_residual_norm,
                 [((M, D), F32, (tm_s, D), _tile_map), ((M, D), BF16, (tm_s, D), _tile_map)],
                 tm=tm_s, tn=D, n_cols=D)

    hdn = _mm("ffn_up", [u2], [(w_up, 0, False), (w_up, ffn // tn_s, False)],
              [(conv_ffn_w, (FFN_CONV, tn_s), _col_map), (conv_ffn_b.reshape(1, -1), (1, tn_s), _col_map)],
              functools.partial(_ep_ffn_up, tiles_per_seq=seq // tm_big),
              [((M, ffn), BF16, (tm_big, tn_s), _tile_map)], tm=tm_big, tn=tn_s, n_cols=ffn,
              scratch=[pltpu.VMEM((SUBLANES, tn_s), F32)])[0]
    out = _mm("ffn_down", [hdn], [(w_down, 0, False)], [(x1, (tm_s, tn_s), _tile_map)], _ep_residual,
              [((M, D), F32, (tm_s, tn_s), _tile_map)], tm=tm_s, tn=tn_s, n_cols=D)[0]
    return out


def kernel(x, norm1_w, w_in, conv_ssm_w, conv_ssm_b, dt_bias, a_log, d_skip, ssm_norm_w, q_norm_w,
           k_norm_w, gate_b, w_ssm_out, w_att_out, w_o, norm2_w, w_up, conv_ffn_w, conv_ffn_b, w_down):
    b, s, d = x.shape
    x2d = x.reshape(b * s, d)
    for l in range(norm1_w.shape[0]):
        x2d = _layer(x2d, b, s, norm1_w[l], w_in[l], conv_ssm_w[l], conv_ssm_b[l], dt_bias[l], a_log[l],
                     d_skip[l], ssm_norm_w[l], q_norm_w[l], k_norm_w[l], gate_b[l], w_ssm_out[l],
                     w_att_out[l], w_o[l], norm2_w[l], w_up[l], conv_ffn_w[l], conv_ffn_b[l], w_down[l])
    return x2d.reshape(b, s, d)
```

```python
import functools
import math

import jax
import jax.numpy as jnp
from jax import lax
from jax.experimental import pallas as pl
from jax.experimental.pallas import tpu as pltpu

EPS = 1e-6
F32 = jnp.float32
BF16 = jnp.bfloat16

SSM_HEAD_DIM = 64
SSM_GROUPS = 8
SSM_STATE = 128
SSM_CONV = 4
CHUNK = 128
SB_HEAD_DIM = 128
FFN_CONV = 3

LANES = 128
SUBLANES = 8
VMEM_LIMIT_BYTES = 56 * 1024 * 1024

F32_EXP2_UNDERFLOW = 150.0


def _sigmoid(x):
    return 1.0 / (1.0 + jnp.exp(-x))


def _silu(x):
    return x * _sigmoid(x)


def _softplus(x):
    return jnp.maximum(x, 0.0) + jnp.log1p(jnp.exp(-jnp.abs(x)))


def _params(sem):
    return pltpu.CompilerParams(dimension_semantics=sem, vmem_limit_bytes=VMEM_LIMIT_BYTES)


def _rmsnorm_kernel(x_ref, w_ref, o_ref):
    x = x_ref[...]
    ms = jnp.mean(x * x, axis=-1, keepdims=True)
    o_ref[...] = (x * lax.rsqrt(ms + EPS) * w_ref[...]).astype(o_ref.dtype)


def _rmsnorm(x2d, w, tm):
    M, D = x2d.shape
    return pl.pallas_call(
        _rmsnorm_kernel,
        grid=(M // tm,),
        in_specs=[pl.BlockSpec((tm, D), lambda i: (i, 0)),
                  pl.BlockSpec((1, D), lambda i: (0, 0))],
        out_specs=pl.BlockSpec((tm, D), lambda i: (i, 0)),
        out_shape=jax.ShapeDtypeStruct((M, D), BF16),
        compiler_params=_params(("arbitrary",)),
        name="rmsnorm",
    )(x2d, w.reshape(1, D))


def _mm_kernel(*refs, n_a, n_mm, n_extra, n_out, stage, trans, rc, epilogue, park):
    a_refs = [refs[k % n_a] for k in range(n_mm)]
    refs = refs[n_a:]
    w_refs = refs[:n_mm]
    extra = refs[n_mm:n_mm + n_extra]
    outs = refs[n_mm + n_extra:n_mm + n_extra + n_out]
    scratch = list(refs[n_mm + n_extra + n_out:])
    i = pl.program_id(1)
    w_bf = [scratch.pop(0) if s else w for w, s in zip(w_refs, stage)]

    @pl.when(i == 0)
    def _():
        for w, wb, s, t in zip(w_refs, w_bf, stage, trans):
            if s:
                wv = w[...].astype(BF16)
                wb[...] = wv.T if t else wv

    tm = a_refs[0].shape[0]
    carry = {}
    parked = [scratch.pop() for _ in range(n_mm)][::-1] if park else None

    def products(r):
        rs = slice(r * rc, (r + 1) * rc)
        out = []
        for k, (a, wb) in enumerate(zip(a_refs, w_bf)):
            acc = jnp.dot(a[rs, :], wb[...], preferred_element_type=F32)
            if park:
                slot = (i + r) % 2
                parked[k][slot] = acc
                acc = parked[k][slot]
            out.append(acc)
        return out

    nxt = products(0)
    for r in range(tm // rc):
        accs = nxt
        if r + 1 < tm // rc:
            nxt = products(r + 1)
        epilogue(accs, extra, outs, scratch, i, slice(r * rc, (r + 1) * rc), carry)


def _mm(name, a_list, w_list, extras, epilogue, outs, *, tm, tn, n_cols, scratch=(), rc=128, park=False):
    M = a_list[0].shape[0]
    rc = min(rc, tm)
    in_specs = []
    for a in a_list:
        in_specs.append(pl.BlockSpec((tm, a.shape[1]), lambda j, i: (i, 0)))
    stage = []
    scratch_shapes = []
    for w, start, t in w_list:
        if t:
            k_dim = w.shape[1]
            in_specs.append(pl.BlockSpec((pl.Element(tn), pl.Element(k_dim)),
                                         functools.partial(
                                             lambda j, i, s: (pl.multiple_of(s + j * tn, SUBLANES), 0), s=start)))
        else:
            k_dim = w.shape[0]
            in_specs.append(pl.BlockSpec((k_dim, tn), functools.partial(lambda j, i, s: (0, j + s), s=start)))
        stage.append(t or w.dtype != BF16)
        if stage[-1]:
            scratch_shapes.append(pltpu.VMEM((k_dim, tn), BF16))
    for _, bs, im in extras:
        in_specs.append(pl.BlockSpec(bs, im))
    out_specs = [pl.BlockSpec(bs, im) for _, _, bs, im in outs]
    out_shape = [jax.ShapeDtypeStruct(s, d) for s, d, _, _ in outs]
    return pl.pallas_call(
        functools.partial(_mm_kernel, n_a=len(a_list), n_mm=len(w_list), n_extra=len(extras), n_out=len(outs),
                          stage=tuple(stage), trans=tuple(t for _, _, t in w_list), rc=rc, epilogue=epilogue, park=park),
        grid=(n_cols // tn, M // tm),
        in_specs=in_specs,
        out_specs=out_specs,
        out_shape=out_shape,
        scratch_shapes=(scratch_shapes + list(scratch)
                        + ([pltpu.VMEM((2, rc, tn), F32) for _ in w_list] if park else [])),
        compiler_params=_params(("arbitrary", "arbitrary")),
        name=name,
    )(*a_list, *[w[0] for w in w_list], *[e[0] for e in extras])


def _tile_map(j, i):
    return (i, j)


def _col_map(j, i):
    return (0, j)


def _fixed_map(j, i):
    return (0, 0)


def _ep_store(accs, extra, outs, scratch, i, rs, carry):
    outs[0][rs, :] = accs[0].astype(outs[0].dtype)


def _ep_silu(accs, extra, outs, scratch, i, rs, carry):
    outs[0][rs, :] = _silu(accs[0]).astype(outs[0].dtype)


def _ep_softplus_bias(accs, extra, outs, scratch, i, rs, carry):
    outs[0][rs, :] = _softplus(accs[0] + extra[0][...])


def _ep_sigmoid_bias(accs, extra, outs, scratch, i, rs, carry):
    outs[0][rs, :] = _sigmoid(accs[0] + extra[0][...]).astype(outs[0].dtype)


def _ep_head_norm(accs, extra, outs, scratch, i, rs, carry, *, scale):
    acc = accs[0]
    w = extra[0][...] * scale
    for c in range(acc.shape[1] // SB_HEAD_DIM):
        g = acc[:, c * SB_HEAD_DIM:(c + 1) * SB_HEAD_DIM]
        ms = jnp.mean(g * g, axis=-1, keepdims=True)
        outs[0][rs, c * SB_HEAD_DIM:(c + 1) * SB_HEAD_DIM] = (g * lax.rsqrt(ms + EPS) * w).astype(outs[0].dtype)


def _causal_conv(acc, cw_ref, cb_ref, tail_ref, i, rs, carry, tm, tiles_per_seq):
    taps = cw_ref.shape[0]
    if rs.start == 0:
        tail = jnp.where(i % tiles_per_seq == 0, 0.0, tail_ref[...])
    else:
        tail = carry["tail"]
    hist = jnp.concatenate([tail, acc], axis=0)
    y = cb_ref[...]
    for k in range(taps - 1):
        y = y + cw_ref[k:k + 1, :] * pltpu.roll(hist, taps - 1 - k, 0)[SUBLANES:, :]
    y = y + cw_ref[taps - 1:taps, :] * acc
    carry["tail"] = acc[acc.shape[0] - SUBLANES:, :]
    if rs.stop == tm:
        tail_ref[...] = carry["tail"]
    return y


def _ep_conv_silu(accs, extra, outs, scratch, i, rs, carry, *, tiles_per_seq):
    y = _causal_conv(accs[0], extra[0], extra[1], scratch[0], i, rs, carry, outs[0].shape[0], tiles_per_seq)
    outs[0][rs, :] = _silu(y).astype(outs[0].dtype)


def _ep_ffn_up(accs, extra, outs, scratch, i, rs, carry, *, tiles_per_seq):
    y = _causal_conv(accs[0], extra[0], extra[1], scratch[0], i, rs, carry, outs[0].shape[0], tiles_per_seq)
    outs[0][rs, :] = (_silu(y) * accs[1]).astype(outs[0].dtype)


def _ep_mix(accs, extra, outs, scratch, i, rs, carry):
    g0 = extra[0][rs, :].astype(F32)
    g1 = extra[1][rs, :].astype(F32)
    outs[0][rs, :] = (g0 * accs[0] + g1 * accs[1]).astype(outs[0].dtype)


def _ep_residual_norm(accs, extra, outs, scratch, i, rs, carry):
    x1 = extra[0][rs, :] + accs[0]
    outs[0][rs, :] = x1
    ms = jnp.mean(x1 * x1, axis=-1, keepdims=True)
    outs[1][rs, :] = (x1 * lax.rsqrt(ms + EPS) * extra[1][...]).astype(outs[1].dtype)


def _ep_residual(accs, extra, outs, scratch, i, rs, carry):
    outs[0][rs, :] = extra[0][rs, :] + accs[0]


def _split3(x):
    h = x.astype(BF16)
    r = x - h.astype(F32)
    m = r.astype(BF16)
    l = (r - m.astype(F32)).astype(BF16)
    return h, m, l


def _ssd_kernel(xbc_ref, dt_ref, zs_ref, alog_ref, dskip_ref, nw_ref, tri_ref,
                o_ref, st, ybuf, *, d_inner, n_groups):
    Q = CHUNK
    P = SSM_HEAD_DIM
    NS = SSM_STATE
    gw = d_inner // n_groups
    pairs = gw // (2 * P)
    c = pl.program_id(1)

    @pl.when(c == 0)
    def _():
        st[...] = jnp.zeros(st.shape, F32)

    dt = dt_ref[...]
    a = dt * (-jnp.exp(alog_ref[...]) * math.log2(math.e))
    tri = tri_ref[...]
    ah, am, al = _split3(a)
    acs = (jnp.dot(tri, ah, preferred_element_type=F32)
           + jnp.dot(tri, am, preferred_element_type=F32)
           + jnp.dot(tri, al, preferred_element_type=F32))
    src_t = (acs - jnp.log2(dt)).T
    wdec_t = (dt * jnp.exp2(acs[Q - 1:Q, :] - acs)).T
    qi = lax.broadcasted_iota(jnp.int32, (Q, Q), 0)
    si = lax.broadcasted_iota(jnp.int32, (Q, Q), 1)
    causal = qi >= si
    low = si < P
    low_b = jnp.where(low[0:1, :], 1.0, 0.0).astype(BF16)
    high_b = jnp.where(low[0:1, :], 0.0, 1.0).astype(BF16)

    for g in range(n_groups):
        b_g = xbc_ref[:, d_inner + g * NS:d_inner + (g + 1) * NS]
        c_g = xbc_ref[:, d_inner + (n_groups + g) * NS:d_inner + (n_groups + g + 1) * NS]
        cb = lax.dot_general(c_g, b_g, (((1,), (1,)), ((), ())), preferred_element_type=F32)
        b_t = b_g.astype(F32).T
        gl = slice(g * gw, (g + 1) * gw)
        y_off = jnp.dot(c_g, st[:, gl].astype(BF16), preferred_element_type=F32)
        for p in range(pairs):
            heads = (g * 2 * pairs + 2 * p, g * 2 * pairs + 2 * p + 1)
            pl_ = slice(g * gw + p * 2 * P, g * gw + (p + 1) * 2 * P)
            x_pair = xbc_ref[:, pl_]
            colb = [jnp.broadcast_to(acs[:, h:h + 1], (Q, Q)) for h in heads]
            ms = [(cb * jnp.exp2(jnp.where(causal, cbh - src_t[h:h + 1, :], -jnp.inf))).astype(BF16)
                  for h, cbh in zip(heads, colb)]
            bw = [(b_t * wdec_t[h:h + 1, :]).astype(BF16) for h in heads]
            lhs = jnp.concatenate([jnp.concatenate(ms, axis=1), jnp.concatenate(bw, axis=1)], axis=0)
            x_bd = jnp.concatenate([x_pair * low_b, x_pair * high_b], axis=0)
            res = jnp.dot(lhs, x_bd, preferred_element_type=F32)
            e_pair = jnp.exp2(jnp.where(low, colb[0], colb[1]))
            ybuf[:, pl_] = (res[:Q] + e_pair * y_off[:, p * 2 * P:(p + 1) * 2 * P]
                            + dskip_ref[:, pl_] * x_pair.astype(F32))
            st[:, pl_] = st[:, pl_] * e_pair[Q - 1:Q, :] + res[Q:]

    for g in range(n_groups):
        gl = slice(g * gw, (g + 1) * gw)
        yg = ybuf[:, gl] * zs_ref[:, gl].astype(F32)
        ms = jnp.mean(yg * yg, axis=-1, keepdims=True)
        o_ref[:, gl] = (yg * lax.rsqrt(ms + EPS) * nw_ref[:, gl]).astype(o_ref.dtype)


def _ssd(xbc, dt, zs, a_log_pad, dskip_row, norm_w, batch, seq):
    M, conv_dim = xbc.shape
    d_inner = zs.shape[1]
    n_groups = SSM_GROUPS
    nc = seq // CHUNK
    tri = (jnp.arange(CHUNK)[:, None] >= jnp.arange(CHUNK)[None, :]).astype(BF16)
    row = lambda b, c: (b * nc + c, 0)
    fixed = lambda b, c: (0, 0)
    return pl.pallas_call(
        functools.partial(_ssd_kernel, d_inner=d_inner, n_groups=n_groups),
        grid=(batch, nc),
        in_specs=[pl.BlockSpec((CHUNK, conv_dim), row),
                  pl.BlockSpec((CHUNK, LANES), row),
                  pl.BlockSpec((CHUNK, d_inner), row),
                  pl.BlockSpec((1, LANES), fixed),
                  pl.BlockSpec((1, d_inner), fixed),
                  pl.BlockSpec((1, d_inner), fixed),
                  pl.BlockSpec((CHUNK, CHUNK), fixed)],
        out_specs=pl.BlockSpec((CHUNK, d_inner), row),
        out_shape=jax.ShapeDtypeStruct((M, d_inner), BF16),
        scratch_shapes=[pltpu.VMEM((SSM_STATE, d_inner), F32),
                        pltpu.VMEM((CHUNK, d_inner), F32)],
        compiler_params=_params(("arbitrary", "arbitrary")),
        name="ssd",
    )(xbc, dt, zs, a_log_pad, dskip_row, norm_w.reshape(1, d_inner), tri)


def _attn_kernel(q_ref, k_ref, v_ref, tri_ref, o_ref, *, tq):
    i = pl.program_id(2)
    hb = q_ref.shape[1] // SB_HEAD_DIM
    tri = tri_ref[...]
    rows = lax.broadcasted_iota(jnp.int32, (tq, tq), 0)
    cols = lax.broadcasted_iota(jnp.int32, (tq, tq), 1)
    mask = cols < rows

    def block(h, kb, run, diagonal):
        ks = pl.multiple_of(kb * tq, tq)
        hs = slice(h * SB_HEAD_DIM, (h + 1) * SB_HEAD_DIM)
        kblk = k_ref[pl.ds(ks, tq), hs]
        vblk = v_ref[pl.ds(ks, tq), hs]
        z = lax.dot_general(q_ref[:, hs], kblk, (((1,), (1,)), ((), ())), preferred_element_type=F32)
        sp = jnp.maximum(z, 0.0) + jnp.log2(1.0 + jnp.exp2(-jnp.abs(z)))
        nlk = jnp.where(mask, sp, 0.0) if diagonal else sp
        later = jnp.dot(nlk.astype(BF16), tri, preferred_element_type=F32)
        att = jnp.exp2(((z - sp) - later) - run)
        if diagonal:
            att = jnp.where(mask, att, 0.0)
        pv = jnp.dot(att.astype(BF16), vblk, preferred_element_type=F32)
        return pv, run + jnp.sum(nlk, axis=-1, keepdims=True)

    def live(runs):
        return jnp.min(functools.reduce(jnp.minimum, runs)) < F32_EXP2_UNDERFLOW

    has_prev = i > 0
    accs, runs = [], []
    for h in range(hb):
        acc, run = block(h, i, jnp.zeros((tq, 1), F32), True)
        pv, run2 = block(h, jnp.maximum(i - 1, 0), run, False)
        accs.append(acc + jnp.where(has_prev, pv, 0.0))
        runs.append(jnp.where(has_prev, run2, run))

    def cond(c):
        return jnp.logical_and(c[0] >= 0, c[1])

    def body(c):
        kb, _, accs, runs = c
        new = [block(h, kb, runs[h], False) for h in range(hb)]
        accs = tuple(a + pv for a, (pv, _) in zip(accs, new))
        runs = tuple(r for _, r in new)
        return kb - 1, live(runs), accs, runs

    _, _, accs, _ = lax.while_loop(cond, body, (i - 2, live(runs), tuple(accs), tuple(runs)))
    for h in range(hb):
        o_ref[:, h * SB_HEAD_DIM:(h + 1) * SB_HEAD_DIM] = accs[h].astype(o_ref.dtype)


def _attention(q, k, v, batch, seq, *, tq, heads_per_step=4):
    M, W = q.shape
    hw = heads_per_step * SB_HEAD_DIM
    nq = seq // tq
    tri = (jnp.arange(tq)[:, None] > jnp.arange(tq)[None, :]).astype(BF16)
    return pl.pallas_call(
        functools.partial(_attn_kernel, tq=tq),
        grid=(batch, W // hw, nq),
        in_specs=[pl.BlockSpec((tq, hw), lambda b, h, i: (b * nq + i, h)),
                  pl.BlockSpec((seq, hw), lambda b, h, i: (b, h)),
                  pl.BlockSpec((seq, hw), lambda b, h, i: (b, h)),
                  pl.BlockSpec((tq, tq), lambda b, h, i: (0, 0))],
        out_specs=pl.BlockSpec((tq, hw), lambda b, h, i: (b * nq + i, h)),
        out_shape=jax.ShapeDtypeStruct((M, W), BF16),
        compiler_params=_params(("arbitrary", "arbitrary", "arbitrary")),
        name="stickbreak_attn",
    )(q, k, v, tri)


def _tile(n, pref):
    t = pref
    while n % t:
        t //= 2
    return t


def _layer(x2d, batch, seq, norm1_w, w_in, conv_ssm_w, conv_ssm_b, dt_bias, a_log, d_skip, ssm_norm_w,
           q_norm_w, k_norm_w, gate_b, w_ssm_out, w_att_out, w_o, norm2_w, w_up, conv_ffn_w,
           conv_ffn_b, w_down):
    M, D = x2d.shape
    d_inner = w_ssm_out.shape[0]
    ssm_heads = dt_bias.shape[0]
    conv_dim = conv_ssm_w.shape[1]
    sb_width = w_att_out.shape[0]
    ffn = w_down.shape[0]
    tm = _tile(seq, 1024)
    tn = 1024

    o_dt = d_inner + conv_dim
    o_q = o_dt + ssm_heads
    assert d_inner % tn == 0 and conv_dim % tn == 0 and sb_width % tn == 0 and o_q % SUBLANES == 0
    w_in_t = jnp.swapaxes(w_in, 0, 1)
    w_dt = jnp.pad(w_in_t[o_dt:o_q], ((0, LANES - ssm_heads), (0, 0))).astype(BF16)
    pad_heads = lambda p: jnp.pad(p, (0, LANES - ssm_heads)).reshape(1, LANES)

    u = _rmsnorm(x2d, norm1_w, tm)

    tm_big = _tile(seq, 2048)

    def proj(name, w, start, n, epilogue, extras, dtype, tn=tn, tm=tm_big, scratch=(), park=False):
        return _mm(name, [u], [(w, start, True)], extras, epilogue, [((M, n), dtype, (tm, tn), _tile_map)],
                   tm=tm, tn=tn, n_cols=n, scratch=scratch, park=park)[0]

    zs = proj("proj_z", w_in_t, 0, d_inner, _ep_silu, [], BF16)
    tn_c = 512
    xbc = proj("proj_xbc", w_in_t, d_inner, conv_dim,
               functools.partial(_ep_conv_silu, tiles_per_seq=seq // tm_big),
               [(conv_ssm_w, (SSM_CONV, tn_c), _col_map), (conv_ssm_b.reshape(1, -1), (1, tn_c), _col_map)],
               BF16, tn=tn_c, scratch=[pltpu.VMEM((SUBLANES, tn_c), F32)], park=True)
    dt = proj("proj_dt", w_dt, 0, LANES, _ep_softplus_bias,
              [(pad_heads(dt_bias), (1, LANES), _col_map)], F32, tn=LANES, tm=tm)
    scale = math.log2(math.e) / math.sqrt(SB_HEAD_DIM)
    head_w = lambda w: (w.reshape(1, SB_HEAD_DIM), (1, SB_HEAD_DIM), _fixed_map)
    q = proj("proj_q", w_in_t, o_q, sb_width, functools.partial(_ep_head_norm, scale=scale),
             [head_w(q_norm_w)], BF16)
    k = proj("proj_k", w_in_t, o_q + sb_width, sb_width, functools.partial(_ep_head_norm, scale=1.0),
             [head_w(k_norm_w)], BF16)
    v = proj("proj_v", w_in_t, o_q + 2 * sb_width, sb_width, _ep_store, [], BF16)
    gates = proj("proj_gate", w_in_t, o_q + 3 * sb_width, 2 * D, _ep_sigmoid_bias,
                 [(gate_b.reshape(1, -1), (1, tn), _col_map)], BF16)

    y_ssm_n = _ssd(xbc, dt, zs, pad_heads(a_log),
                   jnp.repeat(d_skip, SSM_HEAD_DIM).reshape(1, d_inner), ssm_norm_w, batch, seq)
    att = _attention(q, k, v, batch, seq, tq=_tile(seq, 256))

    tn_s = 512
    tm_s = _tile(seq, 512)
    nbm = D // tn_s
    mixed = _mm("mix", [y_ssm_n, att], [(w_ssm_out, 0, False), (w_att_out, 0, False)],
                [(gates, (tm_s, tn_s), _tile_map), (gates, (tm_s, tn_s), lambda j, i: (i, j + nbm))],
                _ep_mix, [((M, D), BF16, (tm_s, tn_s), _tile_map)], tm=tm_s, tn=tn_s, n_cols=D)[0]

    x1, u2 = _mm("out_proj", [mixed], [(w_o.astype(BF16), 0, False)],
                 [(x2d, (tm_s, D), _tile_map), (norm2_w.reshape(1, D), (1, D), _col_map)],
                 _ep_residual_norm,
                 [((M, D), F32, (tm_s, D), _tile_map), ((M, D), BF16, (tm_s, D), _tile_map)],
                 tm=tm_s, tn=D, n_cols=D)

    hdn = _mm("ffn_up", [u2], [(w_up, 0, False), (w_up, ffn // tn_s, False)],
              [(conv_ffn_w, (FFN_CONV, tn_s), _col_map), (conv_ffn_b.reshape(1, -1), (1, tn_s), _col_map)],
              functools.partial(_ep_ffn_up, tiles_per_seq=seq // tm_big),
              [((M, ffn), BF16, (tm_big, tn_s), _tile_map)], tm=tm_big, tn=tn_s, n_cols=ffn,
              scratch=[pltpu.VMEM((SUBLANES, tn_s), F32)])[0]
    out = _mm("ffn_down", [hdn], [(w_down, 0, False)], [(x1, (tm_s, tn_s), _tile_map)], _ep_residual,
              [((M, D), F32, (tm_s, tn_s), _tile_map)], tm=tm_s, tn=tn_s, n_cols=D)[0]
    return out


def kernel(x, norm1_w, w_in, conv_ssm_w, conv_ssm_b, dt_bias, a_log, d_skip, ssm_norm_w, q_norm_w,
           k_norm_w, gate_b, w_ssm_out, w_att_out, w_o, norm2_w, w_up, conv_ffn_w, conv_ffn_b, w_down):
    b, s, d = x.shape
    x2d = x.reshape(b * s, d)
    for l in range(norm1_w.shape[0]):
        x2d = _layer(x2d, b, s, norm1_w[l], w_in[l], conv_ssm_w[l], conv_ssm_b[l], dt_bias[l], a_log[l],
                     d_skip[l], ssm_norm_w[l], q_norm_w[l], k_norm_w[l], gate_b[l], w_ssm_out[l],
                     w_att_out[l], w_o[l], norm2_w[l], w_up[l], conv_ffn_w[l], conv_ffn_b[l], w_down[l])
    return x2d.reshape(b, s, d)
```

```python
import functools
import math

import jax
import jax.numpy as jnp
from jax import lax
from jax.experimental import pallas as pl
from jax.experimental.pallas import tpu as pltpu

EPS = 1e-6
F32 = jnp.float32
BF16 = jnp.bfloat16

SSM_HEAD_DIM = 64
SSM_GROUPS = 8
SSM_STATE = 128
SSM_CONV = 4
CHUNK = 128
SB_HEAD_DIM = 128
FFN_CONV = 3

LANES = 128
SUBLANES = 8
VMEM_LIMIT_BYTES = 56 * 1024 * 1024

F32_EXP2_UNDERFLOW = 150.0


def _sigmoid(x):
    return 1.0 / (1.0 + jnp.exp(-x))


def _silu(x):
    return x * _sigmoid(x)


def _softplus(x):
    return jnp.maximum(x, 0.0) + jnp.log1p(jnp.exp(-jnp.abs(x)))


def _params(sem):
    return pltpu.CompilerParams(dimension_semantics=sem, vmem_limit_bytes=VMEM_LIMIT_BYTES)


def _rmsnorm_kernel(x_ref, w_ref, o_ref):
    x = x_ref[...]
    ms = jnp.mean(x * x, axis=-1, keepdims=True)
    o_ref[...] = (x * lax.rsqrt(ms + EPS) * w_ref[...]).astype(o_ref.dtype)


def _rmsnorm(x2d, w, tm):
    M, D = x2d.shape
    return pl.pallas_call(
        _rmsnorm_kernel,
        grid=(M // tm,),
        in_specs=[pl.BlockSpec((tm, D), lambda i: (i, 0)),
                  pl.BlockSpec((1, D), lambda i: (0, 0))],
        out_specs=pl.BlockSpec((tm, D), lambda i: (i, 0)),
        out_shape=jax.ShapeDtypeStruct((M, D), BF16),
        compiler_params=_params(("arbitrary",)),
        name="rmsnorm",
    )(x2d, w.reshape(1, D))


def _mm_kernel(*refs, n_a, n_mm, n_extra, n_out, stage, trans, rc, epilogue, park):
    a_refs = [refs[k % n_a] for k in range(n_mm)]
    refs = refs[n_a:]
    w_refs = refs[:n_mm]
    extra = refs[n_mm:n_mm + n_extra]
    outs = refs[n_mm + n_extra:n_mm + n_extra + n_out]
    scratch = list(refs[n_mm + n_extra + n_out:])
    i = pl.program_id(1)
    w_bf = [scratch.pop(0) if s else w for w, s in zip(w_refs, stage)]

    @pl.when(i == 0)
    def _():
        for w, wb, s, t in zip(w_refs, w_bf, stage, trans):
            if s:
                wv = w[...].astype(BF16)
                wb[...] = wv.T if t else wv

    tm = a_refs[0].shape[0]
    carry = {}
    parked = [scratch.pop() for _ in range(n_mm)][::-1] if park else None

    def products(r):
        rs = slice(r * rc, (r + 1) * rc)
        out = []
        for k, (a, wb) in enumerate(zip(a_refs, w_bf)):
            acc = jnp.dot(a[rs, :], wb[...], preferred_element_type=F32)
            if park:
                slot = (i + r) % 2
                parked[k][slot] = acc
                acc = parked[k][slot]
            out.append(acc)
        return out

    nxt = products(0)
    for r in range(tm // rc):
        accs = nxt
        if r + 1 < tm // rc:
            nxt = products(r + 1)
        epilogue(accs, extra, outs, scratch, i, slice(r * rc, (r + 1) * rc), carry)


def _mm(name, a_list, w_list, extras, epilogue, outs, *, tm, tn, n_cols, scratch=(), rc=128, park=False):
    M = a_list[0].shape[0]
    rc = min(rc, tm)
    in_specs = []
    for a in a_list:
        in_specs.append(pl.BlockSpec((tm, a.shape[1]), lambda j, i: (i, 0)))
    stage = []
    scratch_shapes = []
    for w, start, t in w_list:
        if t:
            k_dim = w.shape[1]
            in_specs.append(pl.BlockSpec((pl.Element(tn), pl.Element(k_dim)),
                                         functools.partial(
                                             lambda j, i, s: (pl.multiple_of(s + j * tn, SUBLANES), 0), s=start)))
        else:
            k_dim = w.shape[0]
            in_specs.append(pl.BlockSpec((k_dim, tn), functools.partial(lambda j, i, s: (0, j + s), s=start)))
        stage.append(t or w.dtype != BF16)
        if stage[-1]:
            scratch_shapes.append(pltpu.VMEM((k_dim, tn), BF16))
    for _, bs, im in extras:
        in_specs.append(pl.BlockSpec(bs, im))
    out_specs = [pl.BlockSpec(bs, im) for _, _, bs, im in outs]
    out_shape = [jax.ShapeDtypeStruct(s, d) for s, d, _, _ in outs]
    return pl.pallas_call(
        functools.partial(_mm_kernel, n_a=len(a_list), n_mm=len(w_list), n_extra=len(extras), n_out=len(outs),
                          stage=tuple(stage), trans=tuple(t for _, _, t in w_list), rc=rc, epilogue=epilogue, park=park),
        grid=(n_cols // tn, M // tm),
        in_specs=in_specs,
        out_specs=out_specs,
        out_shape=out_shape,
        scratch_shapes=(scratch_shapes + list(scratch)
                        + ([pltpu.VMEM((2, rc, tn), F32) for _ in w_list] if park else [])),
        compiler_params=_params(("arbitrary", "arbitrary")),
        name=name,
    )(*a_list, *[w[0] for w in w_list], *[e[0] for e in extras])


def _tile_map(j, i):
    return (i, j)


def _col_map(j, i):
    return (0, j)


def _fixed_map(j, i):
    return (0, 0)


def _ep_store(accs, extra, outs, scratch, i, rs, carry):
    outs[0][rs, :] = accs[0].astype(outs[0].dtype)


def _ep_silu(accs, extra, outs, scratch, i, rs, carry):
    outs[0][rs, :] = _silu(accs[0]).astype(outs[0].dtype)


def _ep_softplus_bias(accs, extra, outs, scratch, i, rs, carry):
    outs[0][rs, :] = _softplus(accs[0] + extra[0][...])


def _ep_sigmoid_bias(accs, extra, outs, scratch, i, rs, carry):
    outs[0][rs, :] = _sigmoid(accs[0] + extra[0][...]).astype(outs[0].dtype)


def _ep_head_norm(accs, extra, outs, scratch, i, rs, carry, *, scale):
    acc = accs[0]
    w = extra[0][...] * scale
    for c in range(acc.shape[1] // SB_HEAD_DIM):
        g = acc[:, c * SB_HEAD_DIM:(c + 1) * SB_HEAD_DIM]
        ms = jnp.mean(g * g, axis=-1, keepdims=True)
        outs[0][rs, c * SB_HEAD_DIM:(c + 1) * SB_HEAD_DIM] = (g * lax.rsqrt(ms + EPS) * w).astype(outs[0].dtype)


def _causal_conv(acc, cw_ref, cb_ref, tail_ref, i, rs, carry, tm, tiles_per_seq):
    taps = cw_ref.shape[0]
    if rs.start == 0:
        tail = jnp.where(i % tiles_per_seq == 0, 0.0, tail_ref[...])
    else:
        tail = carry["tail"]
    hist = jnp.concatenate([tail, acc], axis=0)
    y = cb_ref[...]
    for k in range(taps - 1):
        y = y + cw_ref[k:k + 1, :] * pltpu.roll(hist, taps - 1 - k, 0)[SUBLANES:, :]
    y = y + cw_ref[taps - 1:taps, :] * acc
    carry["tail"] = acc[acc.shape[0] - SUBLANES:, :]
    if rs.stop == tm:
        tail_ref[...] = carry["tail"]
    return y


def _ep_conv_silu(accs, extra, outs, scratch, i, rs, carry, *, tiles_per_seq):
    y = _causal_conv(accs[0], extra[0], extra[1], scratch[0], i, rs, carry, outs[0].shape[0], tiles_per_seq)
    outs[0][rs, :] = _silu(y).astype(outs[0].dtype)


def _ep_ffn_up(accs, extra, outs, scratch, i, rs, carry, *, tiles_per_seq):
    y = _causal_conv(accs[0], extra[0], extra[1], scratch[0], i, rs, carry, outs[0].shape[0], tiles_per_seq)
    outs[0][rs, :] = (_silu(y) * accs[1]).astype(outs[0].dtype)


def _ep_mix(accs, extra, outs, scratch, i, rs, carry):
    g0 = extra[0][rs, :].astype(F32)
    g1 = extra[1][rs, :].astype(F32)
    outs[0][rs, :] = (g0 * accs[0] + g1 * accs[1]).astype(outs[0].dtype)


def _ep_residual_norm(accs, extra, outs, scratch, i, rs, carry):
    x1 = extra[0][rs, :] + accs[0]
    outs[0][rs, :] = x1
    ms = jnp.mean(x1 * x1, axis=-1, keepdims=True)
    outs[1][rs, :] = (x1 * lax.rsqrt(ms + EPS) * extra[1][...]).astype(outs[1].dtype)


def _ep_residual(accs, extra, outs, scratch, i, rs, carry):
    outs[0][rs, :] = extra[0][rs, :] + accs[0]


def _split3(x):
    h = x.astype(BF16)
    r = x - h.astype(F32)
    m = r.astype(BF16)
    l = (r - m.astype(F32)).astype(BF16)
    return h, m, l


def _ssd_kernel(xbc_ref, dt_ref, zs_ref, alog_ref, dskip_ref, nw_ref, tri_ref,
                o_ref, st, ybuf, *, d_inner, n_groups):
    Q = CHUNK
    P = SSM_HEAD_DIM
    NS = SSM_STATE
    gw = d_inner // n_groups
    pairs = gw // (2 * P)
    c = pl.program_id(1)

    @pl.when(c == 0)
    def _():
        st[...] = jnp.zeros(st.shape, F32)

    dt = dt_ref[...]
    a = dt * (-jnp.exp(alog_ref[...]) * math.log2(math.e))
    tri = tri_ref[...]
    ah, am, al = _split3(a)
    acs = (jnp.dot(tri, ah, preferred_element_type=F32)
           + jnp.dot(tri, am, preferred_element_type=F32)
           + jnp.dot(tri, al, preferred_element_type=F32))
    src_t = (acs - jnp.log2(dt)).T
    wdec_t = (dt * jnp.exp2(acs[Q - 1:Q, :] - acs)).T
    qi = lax.broadcasted_iota(jnp.int32, (Q, Q), 0)
    si = lax.broadcasted_iota(jnp.int32, (Q, Q), 1)
    causal = qi >= si
    low = si < P
    low_b = jnp.where(low[0:1, :], 1.0, 0.0).astype(BF16)
    high_b = jnp.where(low[0:1, :], 0.0, 1.0).astype(BF16)

    for g in range(n_groups):
        b_g = xbc_ref[:, d_inner + g * NS:d_inner + (g + 1) * NS]
        c_g = xbc_ref[:, d_inner + (n_groups + g) * NS:d_inner + (n_groups + g + 1) * NS]
        cb = lax.dot_general(c_g, b_g, (((1,), (1,)), ((), ())), preferred_element_type=F32)
        b_t = b_g.astype(F32).T
        gl = slice(g * gw, (g + 1) * gw)
        y_off = jnp.dot(c_g, st[:, gl].astype(BF16), preferred_element_type=F32)
        for p in range(pairs):
            heads = (g * 2 * pairs + 2 * p, g * 2 * pairs + 2 * p + 1)
            pl_ = slice(g * gw + p * 2 * P, g * gw + (p + 1) * 2 * P)
            x_pair = xbc_ref[:, pl_]
            colb = [jnp.broadcast_to(acs[:, h:h + 1], (Q, Q)) for h in heads]
            ms = [(cb * jnp.exp2(jnp.where(causal, cbh - src_t[h:h + 1, :], -jnp.inf))).astype(BF16)
                  for h, cbh in zip(heads, colb)]
            bw = [(b_t * wdec_t[h:h + 1, :]).astype(BF16) for h in heads]
            lhs = jnp.concatenate([jnp.concatenate(ms, axis=1), jnp.concatenate(bw, axis=1)], axis=0)
            x_bd = jnp.concatenate([x_pair * low_b, x_pair * high_b], axis=0)
            res = jnp.dot(lhs, x_bd, preferred_element_type=F32)
            e_pair = jnp.exp2(jnp.where(low, colb[0], colb[1]))
            ybuf[:, pl_] = (res[:Q] + e_pair * y_off[:, p * 2 * P:(p + 1) * 2 * P]
                            + dskip_ref[:, pl_] * x_pair.astype(F32))
            st[:, pl_] = st[:, pl_] * e_pair[Q - 1:Q, :] + res[Q:]

    for g in range(n_groups):
        gl = slice(g * gw, (g + 1) * gw)
        yg = ybuf[:, gl] * zs_ref[:, gl].astype(F32)
        ms = jnp.mean(yg * yg, axis=-1, keepdims=True)
        o_ref[:, gl] = (yg * lax.rsqrt(ms + EPS) * nw_ref[:, gl]).astype(o_ref.dtype)


def _ssd(xbc, dt, zs, a_log_pad, dskip_row, norm_w, batch, seq):
    M, conv_dim = xbc.shape
    d_inner = zs.shape[1]
    n_groups = SSM_GROUPS
    nc = seq // CHUNK
    tri = (jnp.arange(CHUNK)[:, None] >= jnp.arange(CHUNK)[None, :]).astype(BF16)
    row = lambda b, c: (b * nc + c, 0)
    fixed = lambda b, c: (0, 0)
    return pl.pallas_call(
        functools.partial(_ssd_kernel, d_inner=d_inner, n_groups=n_groups),
        grid=(batch, nc),
        in_specs=[pl.BlockSpec((CHUNK, conv_dim), row),
                  pl.BlockSpec((CHUNK, LANES), row),
                  pl.BlockSpec((CHUNK, d_inner), row),
                  pl.BlockSpec((1, LANES), fixed),
                  pl.BlockSpec((1, d_inner), fixed),
                  pl.BlockSpec((1, d_inner), fixed),
                  pl.BlockSpec((CHUNK, CHUNK), fixed)],
        out_specs=pl.BlockSpec((CHUNK, d_inner), row),
        out_shape=jax.ShapeDtypeStruct((M, d_inner), BF16),
        scratch_shapes=[pltpu.VMEM((SSM_STATE, d_inner), F32),
                        pltpu.VMEM((CHUNK, d_inner), F32)],
        compiler_params=_params(("arbitrary", "arbitrary")),
        name="ssd",
    )(xbc, dt, zs, a_log_pad, dskip_row, norm_w.reshape(1, d_inner), tri)


def _attn_kernel(q_ref, k_ref, v_ref, tri_ref, o_ref, *, tq):
    i = pl.program_id(2)
    hb = q_ref.shape[1] // SB_HEAD_DIM
    tri = tri_ref[...]
    rows = lax.broadcasted_iota(jnp.int32, (tq, tq), 0)
    cols = lax.broadcasted_iota(jnp.int32, (tq, tq), 1)
    mask = cols < rows

    def block(h, kb, run, diagonal):
        ks = pl.multiple_of(kb * tq, tq)
        hs = slice(h * SB_HEAD_DIM, (h + 1) * SB_HEAD_DIM)
        kblk = k_ref[pl.ds(ks, tq), hs]
        vblk = v_ref[pl.ds(ks, tq), hs]
        z = lax.dot_general(q_ref[:, hs], kblk, (((1,), (1,)), ((), ())), preferred_element_type=F32)
        sp = jnp.maximum(z, 0.0) + jnp.log2(1.0 + jnp.exp2(-jnp.abs(z)))
        nlk = jnp.where(mask, sp, 0.0) if diagonal else sp
        later = jnp.dot(nlk.astype(BF16), tri, preferred_element_type=F32)
        att = jnp.exp2(((z - sp) - later) - run)
        if diagonal:
            att = jnp.where(mask, att, 0.0)
        pv = jnp.dot(att.astype(BF16), vblk, preferred_element_type=F32)
        return pv, run + jnp.sum(nlk, axis=-1, keepdims=True)

    def live(runs):
        return jnp.min(functools.reduce(jnp.minimum, runs)) < F32_EXP2_UNDERFLOW

    has_prev = i > 0
    chains = [(h, kb, diag) for h in range(hb) for kb, diag in ((i, True), (jnp.maximum(i - 1, 0), False))]
    hsl = lambda h: slice(h * SB_HEAD_DIM, (h + 1) * SB_HEAD_DIM)
    rowsl = lambda kb: pl.ds(pl.multiple_of(kb * tq, tq), tq)
    zs = [lax.dot_general(q_ref[:, hsl(h)], k_ref[rowsl(kb), hsl(h)], (((1,), (1,)), ((), ())),
                          preferred_element_type=F32) for h, kb, _ in chains]
    sps, sums, laters = [], [], []
    for z, (_, _, diag) in zip(zs, chains):
        sp = jnp.maximum(z, 0.0) + jnp.log2(1.0 + jnp.exp2(-jnp.abs(z)))
        nlk = jnp.where(mask, sp, 0.0) if diag else sp
        sps.append(sp)
        sums.append(jnp.sum(nlk, axis=-1, keepdims=True))
        laters.append(jnp.dot(nlk.astype(BF16), tri, preferred_element_type=F32))
    pvs = []
    for c, (h, kb, diag) in enumerate(chains):
        e = (zs[c] - sps[c]) - laters[c]
        att = jnp.where(mask, jnp.exp2(e), 0.0) if diag else jnp.exp2(e - sums[c - 1])
        pvs.append(jnp.dot(att.astype(BF16), v_ref[rowsl(kb), hsl(h)], preferred_element_type=F32))
    accs = [pvs[2 * h] + jnp.where(has_prev, pvs[2 * h + 1], 0.0) for h in range(hb)]
    runs = [jnp.where(has_prev, sums[2 * h] + sums[2 * h + 1], sums[2 * h]) for h in range(hb)]

    def cond(c):
        return jnp.logical_and(c[0] >= 0, c[1])

    def body(c):
        kb, _, accs, runs = c
        new = [block(h, kb, runs[h], False) for h in range(hb)]
        accs = tuple(a + pv for a, (pv, _) in zip(accs, new))
        runs = tuple(r for _, r in new)
        return kb - 1, live(runs), accs, runs

    _, _, accs, _ = lax.while_loop(cond, body, (i - 2, live(runs), tuple(accs), tuple(runs)))
    for h in range(hb):
        o_ref[:, h * SB_HEAD_DIM:(h + 1) * SB_HEAD_DIM] = accs[h].astype(o_ref.dtype)


def _attention(q, k, v, batch, seq, *, tq, heads_per_step=4):
    M, W = q.shape
    hw = heads_per_step * SB_HEAD_DIM
    nq = seq // tq
    tri = (jnp.arange(tq)[:, None] > jnp.arange(tq)[None, :]).astype(BF16)
    return pl.pallas_call(
        functools.partial(_attn_kernel, tq=tq),
        grid=(batch, W // hw, nq),
        in_specs=[pl.BlockSpec((tq, hw), lambda b, h, i: (b * nq + i, h)),
                  pl.BlockSpec((seq, hw), lambda b, h, i: (b, h)),
                  pl.BlockSpec((seq, hw), lambda b, h, i: (b, h)),
                  pl.BlockSpec((tq, tq), lambda b, h, i: (0, 0))],
        out_specs=pl.BlockSpec((tq, hw), lambda b, h, i: (b * nq + i, h)),
        out_shape=jax.ShapeDtypeStruct((M, W), BF16),
        compiler_params=_params(("arbitrary", "arbitrary", "arbitrary")),
        name="stickbreak_attn",
    )(q, k, v, tri)


def _tile(n, pref):
    t = pref
    while n % t:
        t //= 2
    return t


def _layer(x2d, batch, seq, norm1_w, w_in, conv_ssm_w, conv_ssm_b, dt_bias, a_log, d_skip, ssm_norm_w,
           q_norm_w, k_norm_w, gate_b, w_ssm_out, w_att_out, w_o, norm2_w, w_up, conv_ffn_w,
           conv_ffn_b, w_down):
    M, D = x2d.shape
    d_inner = w_ssm_out.shape[0]
    ssm_heads = dt_bias.shape[0]
    conv_dim = conv_ssm_w.shape[1]
    sb_width = w_att_out.shape[0]
    ffn = w_down.shape[0]
    tm = _tile(seq, 1024)
    tn = 1024

    o_dt = d_inner + conv_dim
    o_q = o_dt + ssm_heads
    assert d_inner % tn == 0 and conv_dim % tn == 0 and sb_width % tn == 0 and o_q % SUBLANES == 0
    w_in_t = jnp.swapaxes(w_in, 0, 1)
    w_dt = jnp.pad(w_in_t[o_dt:o_q], ((0, LANES - ssm_heads), (0, 0))).astype(BF16)
    pad_heads = lambda p: jnp.pad(p, (0, LANES - ssm_heads)).reshape(1, LANES)

    u = _rmsnorm(x2d, norm1_w, tm)

    tm_big = _tile(seq, 2048)

    def proj(name, w, start, n, epilogue, extras, dtype, tn=tn, tm=tm_big, scratch=(), park=False):
        return _mm(name, [u], [(w, start, True)], extras, epilogue, [((M, n), dtype, (tm, tn), _tile_map)],
                   tm=tm, tn=tn, n_cols=n, scratch=scratch, park=park)[0]

    zs = proj("proj_z", w_in_t, 0, d_inner, _ep_silu, [], BF16)
    tn_c = 512
    xbc = proj("proj_xbc", w_in_t, d_inner, conv_dim,
               functools.partial(_ep_conv_silu, tiles_per_seq=seq // tm_big),
               [(conv_ssm_w, (SSM_CONV, tn_c), _col_map), (conv_ssm_b.reshape(1, -1), (1, tn_c), _col_map)],
               BF16, tn=tn_c, scratch=[pltpu.VMEM((SUBLANES, tn_c), F32)], park=True)
    dt = proj("proj_dt", w_dt, 0, LANES, _ep_softplus_bias,
              [(pad_heads(dt_bias), (1, LANES), _col_map)], F32, tn=LANES, tm=tm)
    scale = math.log2(math.e) / math.sqrt(SB_HEAD_DIM)
    head_w = lambda w: (w.reshape(1, SB_HEAD_DIM), (1, SB_HEAD_DIM), _fixed_map)
    q = proj("proj_q", w_in_t, o_q, sb_width, functools.partial(_ep_head_norm, scale=scale),
             [head_w(q_norm_w)], BF16)
    k = proj("proj_k", w_in_t, o_q + sb_width, sb_width, functools.partial(_ep_head_norm, scale=1.0),
             [head_w(k_norm_w)], BF16)
    v = proj("proj_v", w_in_t, o_q + 2 * sb_width, sb_width, _ep_store, [], BF16)
    gates = proj("proj_gate", w_in_t, o_q + 3 * sb_width, 2 * D, _ep_sigmoid_bias,
                 [(gate_b.reshape(1, -1), (1, tn), _col_map)], BF16)

    y_ssm_n = _ssd(xbc, dt, zs, pad_heads(a_log),
                   jnp.repeat(d_skip, SSM_HEAD_DIM).reshape(1, d_inner), ssm_norm_w, batch, seq)
    att = _attention(q, k, v, batch, seq, tq=_tile(seq, 256))

    tn_s = 512
    tm_s = _tile(seq, 512)
    nbm = D // tn_s
    mixed = _mm("mix", [y_ssm_n, att], [(w_ssm_out, 0, False), (w_att_out, 0, False)],
                [(gates, (tm_s, tn_s), _tile_map), (gates, (tm_s, tn_s), lambda j, i: (i, j + nbm))],
                _ep_mix, [((M, D), BF16, (tm_s, tn_s), _tile_map)], tm=tm_s, tn=tn_s, n_cols=D)[0]

    x1, u2 = _mm("out_proj", [mixed], [(w_o.astype(BF16), 0, False)],
                 [(x2d, (tm_s, D), _tile_map), (norm2_w.reshape(1, D), (1, D), _col_map)],
                 _ep_residual_norm,
                 [((M, D), F32, (tm_s, D), _tile_map), ((M, D), BF16, (tm_s, D), _tile_map)],
                 tm=tm_s, tn=D, n_cols=D)

    hdn = _mm("ffn_up", [u2], [(w_up, 0, False), (w_up, ffn // tn_s, False)],
              [(conv_ffn_w, (FFN_CONV, tn_s), _col_map), (conv_ffn_b.reshape(1, -1), (1, tn_s), _col_map)],
              functools.partial(_ep_ffn_up, tiles_per_seq=seq // tm_big),
              [((M, ffn), BF16, (tm_big, tn_s), _tile_map)], tm=tm_big, tn=tn_s, n_cols=ffn,
              scratch=[pltpu.VMEM((SUBLANES, tn_s), F32)])[0]
    out = _mm("ffn_down", [hdn], [(w_down, 0, False)], [(x1, (tm_s, tn_s), _tile_map)], _ep_residual,
              [((M, D), F32, (tm_s, tn_s), _tile_map)], tm=tm_s, tn=tn_s, n_cols=D)[0]
    return out


def kernel(x, norm1_w, w_in, conv_ssm_w, conv_ssm_b, dt_bias, a_log, d_skip, ssm_norm_w, q_norm_w,
           k_norm_w, gate_b, w_ssm_out, w_att_out, w_o, norm2_w, w_up, conv_ffn_w, conv_ffn_b, w_down):
    b, s, d = x.shape
    x2d = x.reshape(b * s, d)
    for l in range(norm1_w.shape[0]):
        x2d = _layer(x2d, b, s, norm1_w[l], w_in[l], conv_ssm_w[l], conv_ssm_b[l], dt_bias[l], a_log[l],
                     d_skip[l], ssm_norm_w[l], q_norm_w[l], k_norm_w[l], gate_b[l], w_ssm_out[l],
                     w_att_out[l], w_o[l], norm2_w[l], w_up[l], conv_ffn_w[l], conv_ffn_b[l], w_down[l])
    return x2d.reshape(b, s, d)
```

```python
import functools
import math

import jax
import jax.numpy as jnp
from jax import lax
from jax.experimental import pallas as pl
from jax.experimental.pallas import tpu as pltpu

EPS = 1e-6
F32 = jnp.float32
BF16 = jnp.bfloat16

SSM_HEAD_DIM = 64
SSM_GROUPS = 8
SSM_STATE = 128
SSM_CONV = 4
CHUNK = 128
SB_HEAD_DIM = 128
FFN_CONV = 3

LANES = 128
SUBLANES = 8
VMEM_LIMIT_BYTES = 56 * 1024 * 1024

F32_EXP2_UNDERFLOW = 150.0


def _sigmoid(x):
    return 1.0 / (1.0 + jnp.exp(-x))


def _silu(x):
    return x * _sigmoid(x)


def _softplus(x):
    return jnp.maximum(x, 0.0) + jnp.log1p(jnp.exp(-jnp.abs(x)))


def _params(sem):
    return pltpu.CompilerParams(dimension_semantics=sem, vmem_limit_bytes=VMEM_LIMIT_BYTES)


def _rmsnorm_dt_kernel(x_ref, w_ref, wdt_ref, bdt_ref, o_ref, dt_ref):
    x = x_ref[...]
    ms = jnp.mean(x * x, axis=-1, keepdims=True)
    u = (x * lax.rsqrt(ms + EPS) * w_ref[...]).astype(o_ref.dtype)
    o_ref[...] = u
    acc = lax.dot_general(u, wdt_ref[...], (((1,), (1,)), ((), ())), preferred_element_type=F32)
    dt_ref[...] = _softplus(acc + bdt_ref[...])


def _rmsnorm_dt(x2d, w, w_dt, b_dt, tm):
    M, D = x2d.shape
    return pl.pallas_call(
        _rmsnorm_dt_kernel,
        grid=(M // tm,),
        in_specs=[pl.BlockSpec((tm, D), lambda i: (i, 0)),
                  pl.BlockSpec((1, D), lambda i: (0, 0)),
                  pl.BlockSpec((LANES, D), lambda i: (0, 0)),
                  pl.BlockSpec((1, LANES), lambda i: (0, 0))],
        out_specs=[pl.BlockSpec((tm, D), lambda i: (i, 0)),
                   pl.BlockSpec((tm, LANES), lambda i: (i, 0))],
        out_shape=[jax.ShapeDtypeStruct((M, D), BF16), jax.ShapeDtypeStruct((M, LANES), F32)],
        compiler_params=_params(("arbitrary",)),
        name="rmsnorm_dt",
    )(x2d, w.reshape(1, D), w_dt, b_dt)


def _mm_kernel(*refs, n_a, n_mm, n_extra, n_out, stage, trans, rc, epilogue, park):
    a_refs = [refs[k % n_a] for k in range(n_mm)]
    refs = refs[n_a:]
    w_refs = refs[:n_mm]
    extra = refs[n_mm:n_mm + n_extra]
    outs = refs[n_mm + n_extra:n_mm + n_extra + n_out]
    scratch = list(refs[n_mm + n_extra + n_out:])
    i = pl.program_id(1)
    w_bf = [scratch.pop(0) if s else w for w, s in zip(w_refs, stage)]

    @pl.when(i == 0)
    def _():
        for w, wb, s, t in zip(w_refs, w_bf, stage, trans):
            if s:
                wv = w[...].astype(BF16)
                wb[...] = wv.T if t else wv

    tm = a_refs[0].shape[0]
    carry = {}
    parked = [scratch.pop() for _ in range(n_mm)][::-1] if park else None

    def products(r):
        rs = slice(r * rc, (r + 1) * rc)
        out = []
        for k, (a, wb) in enumerate(zip(a_refs, w_bf)):
            acc = jnp.dot(a[rs, :], wb[...], preferred_element_type=F32)
            if park:
                slot = (i + r) % 2
                parked[k][slot] = acc
                acc = parked[k][slot]
            out.append(acc)
        return out

    nxt = products(0)
    for r in range(tm // rc):
        accs = nxt
        if r + 1 < tm // rc:
            nxt = products(r + 1)
        epilogue(accs, extra, outs, scratch, i, slice(r * rc, (r + 1) * rc), carry)


def _mm(name, a_list, w_list, extras, epilogue, outs, *, tm, tn, n_cols, scratch=(), rc=128, park=False):
    M = a_list[0].shape[0]
    rc = min(rc, tm)
    in_specs = []
    for a in a_list:
        in_specs.append(pl.BlockSpec((tm, a.shape[1]), lambda j, i: (i, 0)))
    stage = []
    scratch_shapes = []
    for w, start, t in w_list:
        if t:
            k_dim = w.shape[1]
            in_specs.append(pl.BlockSpec((pl.Element(tn), pl.Element(k_dim)),
                                         functools.partial(
                                             lambda j, i, s: (pl.multiple_of(s + j * tn, SUBLANES), 0), s=start)))
        else:
            k_dim = w.shape[0]
            in_specs.append(pl.BlockSpec((k_dim, tn), functools.partial(lambda j, i, s: (0, j + s), s=start)))
        stage.append(t or w.dtype != BF16)
        if stage[-1]:
            scratch_shapes.append(pltpu.VMEM((k_dim, tn), BF16))
    for _, bs, im in extras:
        in_specs.append(pl.BlockSpec(bs, im))
    out_specs = [pl.BlockSpec(bs, im) for _, _, bs, im in outs]
    out_shape = [jax.ShapeDtypeStruct(s, d) for s, d, _, _ in outs]
    return pl.pallas_call(
        functools.partial(_mm_kernel, n_a=len(a_list), n_mm=len(w_list), n_extra=len(extras), n_out=len(outs),
                          stage=tuple(stage), trans=tuple(t for _, _, t in w_list), rc=rc, epilogue=epilogue, park=park),
        grid=(n_cols // tn, M // tm),
        in_specs=in_specs,
        out_specs=out_specs,
        out_shape=out_shape,
        scratch_shapes=(scratch_shapes + list(scratch)
                        + ([pltpu.VMEM((2, rc, tn), F32) for _ in w_list] if park else [])),
        compiler_params=_params(("arbitrary", "arbitrary")),
        name=name,
    )(*a_list, *[w[0] for w in w_list], *[e[0] for e in extras])


def _tile_map(j, i):
    return (i, j)


def _col_map(j, i):
    return (0, j)


def _fixed_map(j, i):
    return (0, 0)


def _ep_store(accs, extra, outs, scratch, i, rs, carry):
    outs[0][rs, :] = accs[0].astype(outs[0].dtype)


def _ep_silu(accs, extra, outs, scratch, i, rs, carry):
    outs[0][rs, :] = _silu(accs[0]).astype(outs[0].dtype)


def _ep_sigmoid_bias(accs, extra, outs, scratch, i, rs, carry):
    outs[0][rs, :] = _sigmoid(accs[0] + extra[0][...]).astype(outs[0].dtype)


def _ep_head_norm(accs, extra, outs, scratch, i, rs, carry, *, scale):
    acc = accs[0]
    w = extra[0][...] * scale
    for c in range(acc.shape[1] // SB_HEAD_DIM):
        g = acc[:, c * SB_HEAD_DIM:(c + 1) * SB_HEAD_DIM]
        ms = jnp.mean(g * g, axis=-1, keepdims=True)
        outs[0][rs, c * SB_HEAD_DIM:(c + 1) * SB_HEAD_DIM] = (g * lax.rsqrt(ms + EPS) * w).astype(outs[0].dtype)


def _causal_conv(acc, cw_ref, cb_ref, tail_ref, i, rs, carry, tm, tiles_per_seq):
    taps = cw_ref.shape[0]
    if rs.start == 0:
        tail = jnp.where(i % tiles_per_seq == 0, 0.0, tail_ref[...])
    else:
        tail = carry["tail"]
    hist = jnp.concatenate([tail, acc], axis=0)
    y = cb_ref[...]
    for k in range(taps - 1):
        y = y + cw_ref[k:k + 1, :] * pltpu.roll(hist, taps - 1 - k, 0)[SUBLANES:, :]
    y = y + cw_ref[taps - 1:taps, :] * acc
    carry["tail"] = acc[acc.shape[0] - SUBLANES:, :]
    if rs.stop == tm:
        tail_ref[...] = carry["tail"]
    return y


def _ep_conv_silu(accs, extra, outs, scratch, i, rs, carry, *, tiles_per_seq):
    y = _causal_conv(accs[0], extra[0], extra[1], scratch[0], i, rs, carry, outs[0].shape[0], tiles_per_seq)
    outs[0][rs, :] = _silu(y).astype(outs[0].dtype)


def _ep_ffn_up(accs, extra, outs, scratch, i, rs, carry, *, tiles_per_seq):
    y = _causal_conv(accs[0], extra[0], extra[1], scratch[0], i, rs, carry, outs[0].shape[0], tiles_per_seq)
    outs[0][rs, :] = (_silu(y) * accs[1]).astype(outs[0].dtype)


def _ep_mix(accs, extra, outs, scratch, i, rs, carry):
    g0 = extra[0][rs, :].astype(F32)
    g1 = extra[1][rs, :].astype(F32)
    outs[0][rs, :] = (g0 * accs[0] + g1 * accs[1]).astype(outs[0].dtype)


def _ep_residual_norm(accs, extra, outs, scratch, i, rs, carry):
    x1 = extra[0][rs, :] + accs[0]
    outs[0][rs, :] = x1
    ms = jnp.mean(x1 * x1, axis=-1, keepdims=True)
    outs[1][rs, :] = (x1 * lax.rsqrt(ms + EPS) * extra[1][...]).astype(outs[1].dtype)


def _ep_residual(accs, extra, outs, scratch, i, rs, carry):
    outs[0][rs, :] = extra[0][rs, :] + accs[0]


def _split3(x):
    h = x.astype(BF16)
    r = x - h.astype(F32)
    m = r.astype(BF16)
    l = (r - m.astype(F32)).astype(BF16)
    return h, m, l


def _ssd_kernel(xbc_ref, dt_ref, zs_ref, alog_ref, dskip_ref, nw_ref, tri_ref,
                o_ref, st, ybuf, *, d_inner, n_groups):
    Q = CHUNK
    P = SSM_HEAD_DIM
    NS = SSM_STATE
    gw = d_inner // n_groups
    pairs = gw // (2 * P)
    c = pl.program_id(1)

    @pl.when(c == 0)
    def _():
        st[...] = jnp.zeros(st.shape, F32)

    dt = dt_ref[...]
    a = dt * (-jnp.exp(alog_ref[...]) * math.log2(math.e))
    tri = tri_ref[...]
    ah, am, al = _split3(a)
    acs = (jnp.dot(tri, ah, preferred_element_type=F32)
           + jnp.dot(tri, am, preferred_element_type=F32)
           + jnp.dot(tri, al, preferred_element_type=F32))
    src_t = (acs - jnp.log2(dt)).T
    wdec_t = (dt * jnp.exp2(acs[Q - 1:Q, :] - acs)).T
    qi = lax.broadcasted_iota(jnp.int32, (Q, Q), 0)
    si = lax.broadcasted_iota(jnp.int32, (Q, Q), 1)
    causal = qi >= si
    low = si < P
    low_b = jnp.where(low[0:1, :], 1.0, 0.0).astype(BF16)
    high_b = jnp.where(low[0:1, :], 0.0, 1.0).astype(BF16)

    for g in range(n_groups):
        b_g = xbc_ref[:, d_inner + g * NS:d_inner + (g + 1) * NS]
        c_g = xbc_ref[:, d_inner + (n_groups + g) * NS:d_inner + (n_groups + g + 1) * NS]
        cb = lax.dot_general(c_g, b_g, (((1,), (1,)), ((), ())), preferred_element_type=F32)
        b_t = b_g.astype(F32).T
        gl = slice(g * gw, (g + 1) * gw)
        y_off = jnp.dot(c_g, st[:, gl].astype(BF16), preferred_element_type=F32)
        for p in range(pairs):
            heads = (g * 2 * pairs + 2 * p, g * 2 * pairs + 2 * p + 1)
            pl_ = slice(g * gw + p * 2 * P, g * gw + (p + 1) * 2 * P)
            x_pair = xbc_ref[:, pl_]
            colb = [jnp.broadcast_to(acs[:, h:h + 1], (Q, Q)) for h in heads]
            ms = [(cb * jnp.exp2(jnp.where(causal, cbh - src_t[h:h + 1, :], -jnp.inf))).astype(BF16)
                  for h, cbh in zip(heads, colb)]
            bw = [(b_t * wdec_t[h:h + 1, :]).astype(BF16) for h in heads]
            lhs = jnp.concatenate([jnp.concatenate(ms, axis=1), jnp.concatenate(bw, axis=1)], axis=0)
            x_bd = jnp.concatenate([x_pair * low_b, x_pair * high_b], axis=0)
            res = jnp.dot(lhs, x_bd, preferred_element_type=F32)
            e_pair = jnp.exp2(jnp.where(low, colb[0], colb[1]))
            ybuf[:, pl_] = (res[:Q] + e_pair * y_off[:, p * 2 * P:(p + 1) * 2 * P]
                            + dskip_ref[:, pl_] * x_pair.astype(F32))
            st[:, pl_] = st[:, pl_] * e_pair[Q - 1:Q, :] + res[Q:]

    for g in range(n_groups):
        gl = slice(g * gw, (g + 1) * gw)
        yg = ybuf[:, gl] * zs_ref[:, gl].astype(F32)
        ms = jnp.mean(yg * yg, axis=-1, keepdims=True)
        o_ref[:, gl] = (yg * lax.rsqrt(ms + EPS) * nw_ref[:, gl]).astype(o_ref.dtype)


def _ssd(xbc, dt, zs, a_log_pad, dskip_row, norm_w, batch, seq):
    M, conv_dim = xbc.shape
    d_inner = zs.shape[1]
    n_groups = SSM_GROUPS
    nc = seq // CHUNK
    tri = (jnp.arange(CHUNK)[:, None] >= jnp.arange(CHUNK)[None, :]).astype(BF16)
    row = lambda b, c: (b * nc + c, 0)
    fixed = lambda b, c: (0, 0)
    return pl.pallas_call(
        functools.partial(_ssd_kernel, d_inner=d_inner, n_groups=n_groups),
        grid=(batch, nc),
        in_specs=[pl.BlockSpec((CHUNK, conv_dim), row),
                  pl.BlockSpec((CHUNK, LANES), row),
                  pl.BlockSpec((CHUNK, d_inner), row),
                  pl.BlockSpec((1, LANES), fixed),
                  pl.BlockSpec((1, d_inner), fixed),
                  pl.BlockSpec((1, d_inner), fixed),
                  pl.BlockSpec((CHUNK, CHUNK), fixed)],
        out_specs=pl.BlockSpec((CHUNK, d_inner), row),
        out_shape=jax.ShapeDtypeStruct((M, d_inner), BF16),
        scratch_shapes=[pltpu.VMEM((SSM_STATE, d_inner), F32),
                        pltpu.VMEM((CHUNK, d_inner), F32)],
        compiler_params=_params(("arbitrary", "arbitrary")),
        name="ssd",
    )(xbc, dt, zs, a_log_pad, dskip_row, norm_w.reshape(1, d_inner), tri)


def _attn_kernel(q_ref, k_ref, v_ref, tri_ref, o_ref, *, tq):
    i = pl.program_id(2)
    hb = q_ref.shape[1] // SB_HEAD_DIM
    tri = tri_ref[...]
    rows = lax.broadcasted_iota(jnp.int32, (tq, tq), 0)
    cols = lax.broadcasted_iota(jnp.int32, (tq, tq), 1)
    mask = cols < rows

    def block(h, kb, run, diagonal):
        ks = pl.multiple_of(kb * tq, tq)
        hs = slice(h * SB_HEAD_DIM, (h + 1) * SB_HEAD_DIM)
        kblk = k_ref[pl.ds(ks, tq), hs]
        vblk = v_ref[pl.ds(ks, tq), hs]
        z = lax.dot_general(q_ref[:, hs], kblk, (((1,), (1,)), ((), ())), preferred_element_type=F32)
        sp = jnp.maximum(z, 0.0) + jnp.log2(1.0 + jnp.exp2(-jnp.abs(z)))
        nlk = jnp.where(mask, sp, 0.0) if diagonal else sp
        later = jnp.dot(nlk.astype(BF16), tri, preferred_element_type=F32)
        att = jnp.exp2(((z - sp) - later) - run)
        if diagonal:
            att = jnp.where(mask, att, 0.0)
        pv = jnp.dot(att.astype(BF16), vblk, preferred_element_type=F32)
        return pv, run + jnp.sum(nlk, axis=-1, keepdims=True)

    def live(runs):
        return jnp.min(functools.reduce(jnp.minimum, runs)) < F32_EXP2_UNDERFLOW

    has_prev = i > 0
    chains = [(h, kb, diag) for h in range(hb) for kb, diag in ((i, True), (jnp.maximum(i - 1, 0), False))]
    hsl = lambda h: slice(h * SB_HEAD_DIM, (h + 1) * SB_HEAD_DIM)
    rowsl = lambda kb: pl.ds(pl.multiple_of(kb * tq, tq), tq)
    zs = [lax.dot_general(q_ref[:, hsl(h)], k_ref[rowsl(kb), hsl(h)], (((1,), (1,)), ((), ())),
                          preferred_element_type=F32) for h, kb, _ in chains]
    sps, sums, laters = [], [], []
    for z, (_, _, diag) in zip(zs, chains):
        sp = jnp.maximum(z, 0.0) + jnp.log2(1.0 + jnp.exp2(-jnp.abs(z)))
        nlk = jnp.where(mask, sp, 0.0) if diag else sp
        sps.append(sp)
        sums.append(jnp.sum(nlk, axis=-1, keepdims=True))
        laters.append(jnp.dot(nlk.astype(BF16), tri, preferred_element_type=F32))
    pvs = []
    for c, (h, kb, diag) in enumerate(chains):
        e = (zs[c] - sps[c]) - laters[c]
        att = jnp.where(mask, jnp.exp2(e), 0.0) if diag else jnp.exp2(e - sums[c - 1])
        pvs.append(jnp.dot(att.astype(BF16), v_ref[rowsl(kb), hsl(h)], preferred_element_type=F32))
    accs = [pvs[2 * h] + jnp.where(has_prev, pvs[2 * h + 1], 0.0) for h in range(hb)]
    runs = [jnp.where(has_prev, sums[2 * h] + sums[2 * h + 1], sums[2 * h]) for h in range(hb)]

    def cond(c):
        return jnp.logical_and(c[0] >= 0, c[1])

    def body(c):
        kb, _, accs, runs = c
        new = [block(h, kb, runs[h], False) for h in range(hb)]
        accs = tuple(a + pv for a, (pv, _) in zip(accs, new))
        runs = tuple(r for _, r in new)
        return kb - 1, live(runs), accs, runs

    _, _, accs, _ = lax.while_loop(cond, body, (i - 2, live(runs), tuple(accs), tuple(runs)))
    for h in range(hb):
        o_ref[:, h * SB_HEAD_DIM:(h + 1) * SB_HEAD_DIM] = accs[h].astype(o_ref.dtype)


def _attention(q, k, v, batch, seq, *, tq, heads_per_step=4):
    M, W = q.shape
    hw = heads_per_step * SB_HEAD_DIM
    nq = seq // tq
    tri = (jnp.arange(tq)[:, None] > jnp.arange(tq)[None, :]).astype(BF16)
    return pl.pallas_call(
        functools.partial(_attn_kernel, tq=tq),
        grid=(batch, W // hw, nq),
        in_specs=[pl.BlockSpec((tq, hw), lambda b, h, i: (b * nq + i, h)),
                  pl.BlockSpec((seq, hw), lambda b, h, i: (b, h)),
                  pl.BlockSpec((seq, hw), lambda b, h, i: (b, h)),
                  pl.BlockSpec((tq, tq), lambda b, h, i: (0, 0))],
        out_specs=pl.BlockSpec((tq, hw), lambda b, h, i: (b * nq + i, h)),
        out_shape=jax.ShapeDtypeStruct((M, W), BF16),
        compiler_params=_params(("arbitrary", "arbitrary", "arbitrary")),
        name="stickbreak_attn",
    )(q, k, v, tri)


def _tile(n, pref):
    t = pref
    while n % t:
        t //= 2
    return t


def _layer(x2d, batch, seq, norm1_w, w_in, conv_ssm_w, conv_ssm_b, dt_bias, a_log, d_skip, ssm_norm_w,
           q_norm_w, k_norm_w, gate_b, w_ssm_out, w_att_out, w_o, norm2_w, w_up, conv_ffn_w,
           conv_ffn_b, w_down):
    M, D = x2d.shape
    d_inner = w_ssm_out.shape[0]
    ssm_heads = dt_bias.shape[0]
    conv_dim = conv_ssm_w.shape[1]
    sb_width = w_att_out.shape[0]
    ffn = w_down.shape[0]
    tm = _tile(seq, 1024)
    tn = 1024

    o_dt = d_inner + conv_dim
    o_q = o_dt + ssm_heads
    assert d_inner % tn == 0 and conv_dim % tn == 0 and sb_width % tn == 0 and o_q % SUBLANES == 0
    w_in_t = jnp.swapaxes(w_in, 0, 1)
    w_dt = jnp.pad(w_in_t[o_dt:o_q], ((0, LANES - ssm_heads), (0, 0))).astype(BF16)
    pad_heads = lambda p: jnp.pad(p, (0, LANES - ssm_heads)).reshape(1, LANES)

    u, dt = _rmsnorm_dt(x2d, norm1_w, w_dt, pad_heads(dt_bias), tm)

    tm_big = _tile(seq, 2048)

    def proj(name, w, start, n, epilogue, extras, dtype, tn=tn, tm=tm_big, scratch=(), park=False):
        return _mm(name, [u], [(w, start, True)], extras, epilogue, [((M, n), dtype, (tm, tn), _tile_map)],
                   tm=tm, tn=tn, n_cols=n, scratch=scratch, park=park)[0]

    zs = proj("proj_z", w_in_t, 0, d_inner, _ep_silu, [], BF16)
    tn_c = 512
    xbc = proj("proj_xbc", w_in_t, d_inner, conv_dim,
               functools.partial(_ep_conv_silu, tiles_per_seq=seq // tm_big),
               [(conv_ssm_w, (SSM_CONV, tn_c), _col_map), (conv_ssm_b.reshape(1, -1), (1, tn_c), _col_map)],
               BF16, tn=tn_c, scratch=[pltpu.VMEM((SUBLANES, tn_c), F32)], park=True)
    scale = math.log2(math.e) / math.sqrt(SB_HEAD_DIM)
    head_w = lambda w: (w.reshape(1, SB_HEAD_DIM), (1, SB_HEAD_DIM), _fixed_map)
    q = proj("proj_q", w_in_t, o_q, sb_width, functools.partial(_ep_head_norm, scale=scale),
             [head_w(q_norm_w)], BF16)
    k = proj("proj_k", w_in_t, o_q + sb_width, sb_width, functools.partial(_ep_head_norm, scale=1.0),
             [head_w(k_norm_w)], BF16)
    v = proj("proj_v", w_in_t, o_q + 2 * sb_width, sb_width, _ep_store, [], BF16)
    gates = proj("proj_gate", w_in_t, o_q + 3 * sb_width, 2 * D, _ep_sigmoid_bias,
                 [(gate_b.reshape(1, -1), (1, tn), _col_map)], BF16)

    y_ssm_n = _ssd(xbc, dt, zs, pad_heads(a_log),
                   jnp.repeat(d_skip, SSM_HEAD_DIM).reshape(1, d_inner), ssm_norm_w, batch, seq)
    att = _attention(q, k, v, batch, seq, tq=_tile(seq, 256))

    tn_s = 512
    tm_s = _tile(seq, 512)
    nbm = D // tn_s
    mixed = _mm("mix", [y_ssm_n, att], [(w_ssm_out, 0, False), (w_att_out, 0, False)],
                [(gates, (tm_s, tn_s), _tile_map), (gates, (tm_s, tn_s), lambda j, i: (i, j + nbm))],
                _ep_mix, [((M, D), BF16, (tm_s, tn_s), _tile_map)], tm=tm_s, tn=tn_s, n_cols=D)[0]

    x1, u2 = _mm("out_proj", [mixed], [(w_o.astype(BF16), 0, False)],
                 [(x2d, (tm_s, D), _tile_map), (norm2_w.reshape(1, D), (1, D), _col_map)],
                 _ep_residual_norm,
                 [((M, D), F32, (tm_s, D), _tile_map), ((M, D), BF16, (tm_s, D), _tile_map)],
                 tm=tm_s, tn=D, n_cols=D)

    hdn = _mm("ffn_up", [u2], [(w_up, 0, False), (w_up, ffn // tn_s, False)],
              [(conv_ffn_w, (FFN_CONV, tn_s), _col_map), (conv_ffn_b.reshape(1, -1), (1, tn_s), _col_map)],
              functools.partial(_ep_ffn_up, tiles_per_seq=seq // tm_big),
              [((M, ffn), BF16, (tm_big, tn_s), _tile_map)], tm=tm_big, tn=tn_s, n_cols=ffn,
              scratch=[pltpu.VMEM((SUBLANES, tn_s), F32)])[0]
    out = _mm("ffn_down", [hdn], [(w_down, 0, False)], [(x1, (tm_s, tn_s), _tile_map)], _ep_residual,
              [((M, D), F32, (tm_s, tn_s), _tile_map)], tm=tm_s, tn=tn_s, n_cols=D)[0]
    return out


def kernel(x, norm1_w, w_in, conv_ssm_w, conv_ssm_b, dt_bias, a_log, d_skip, ssm_norm_w, q_norm_w,
           k_norm_w, gate_b, w_ssm_out, w_att_out, w_o, norm2_w, w_up, conv_ffn_w, conv_ffn_b, w_down):
    b, s, d = x.shape
    x2d = x.reshape(b * s, d)
    for l in range(norm1_w.shape[0]):
        x2d = _layer(x2d, b, s, norm1_w[l], w_in[l], conv_ssm_w[l], conv_ssm_b[l], dt_bias[l], a_log[l],
                     d_skip[l], ssm_norm_w[l], q_norm_w[l], k_norm_w[l], gate_b[l], w_ssm_out[l],
                     w_att_out[l], w_o[l], norm2_w[l], w_up[l], conv_ffn_w[l], conv_ffn_b[l], w_down[l])
    return x2d.reshape(b, s, d)
```

```python
import functools
import math

import jax
import jax.numpy as jnp
from jax import lax
from jax.experimental import pallas as pl
from jax.experimental.pallas import tpu as pltpu

EPS = 1e-6
F32 = jnp.float32
BF16 = jnp.bfloat16

SSM_HEAD_DIM = 64
SSM_GROUPS = 8
SSM_STATE = 128
SSM_CONV = 4
CHUNK = 128
SB_HEAD_DIM = 128
FFN_CONV = 3

LANES = 128
SUBLANES = 8
VMEM_LIMIT_BYTES = 56 * 1024 * 1024

F32_EXP2_UNDERFLOW = 150.0


def _sigmoid(x):
    return 1.0 / (1.0 + jnp.exp(-x))


def _silu(x):
    h = 0.5 * x
    return h + h * jnp.tanh(h)


def _softplus(x):
    return jnp.maximum(x, 0.0) + jnp.log1p(jnp.exp(-jnp.abs(x)))


def _params(sem):
    return pltpu.CompilerParams(dimension_semantics=sem, vmem_limit_bytes=VMEM_LIMIT_BYTES)


def _rmsnorm_dt_kernel(x_ref, w_ref, wdt_ref, bdt_ref, o_ref, dt_ref):
    x = x_ref[...]
    ms = jnp.mean(x * x, axis=-1, keepdims=True)
    u = (x * lax.rsqrt(ms + EPS) * w_ref[...]).astype(o_ref.dtype)
    o_ref[...] = u
    acc = lax.dot_general(u, wdt_ref[...], (((1,), (1,)), ((), ())), preferred_element_type=F32)
    dt_ref[...] = _softplus(acc + bdt_ref[...])


def _rmsnorm_dt(x2d, w, w_dt, b_dt, tm):
    M, D = x2d.shape
    return pl.pallas_call(
        _rmsnorm_dt_kernel,
        grid=(M // tm,),
        in_specs=[pl.BlockSpec((tm, D), lambda i: (i, 0)),
                  pl.BlockSpec((1, D), lambda i: (0, 0)),
                  pl.BlockSpec((LANES, D), lambda i: (0, 0)),
                  pl.BlockSpec((1, LANES), lambda i: (0, 0))],
        out_specs=[pl.BlockSpec((tm, D), lambda i: (i, 0)),
                   pl.BlockSpec((tm, LANES), lambda i: (i, 0))],
        out_shape=[jax.ShapeDtypeStruct((M, D), BF16), jax.ShapeDtypeStruct((M, LANES), F32)],
        compiler_params=_params(("arbitrary",)),
        name="rmsnorm_dt",
    )(x2d, w.reshape(1, D), w_dt, b_dt)


def _mm_kernel(*refs, n_a, n_mm, n_extra, n_out, stage, trans, rc, epilogue, park):
    a_refs = [refs[k % n_a] for k in range(n_mm)]
    refs = refs[n_a:]
    w_refs = refs[:n_mm]
    extra = refs[n_mm:n_mm + n_extra]
    outs = refs[n_mm + n_extra:n_mm + n_extra + n_out]
    scratch = list(refs[n_mm + n_extra + n_out:])
    i = pl.program_id(1)
    w_bf = [scratch.pop(0) if s else w for w, s in zip(w_refs, stage)]

    @pl.when(i == 0)
    def _():
        for w, wb, s, t in zip(w_refs, w_bf, stage, trans):
            if s:
                wv = w[...].astype(BF16)
                wb[...] = wv.T if t else wv

    tm = a_refs[0].shape[0]
    carry = {}
    parked = [scratch.pop() for _ in range(n_mm)][::-1] if park else None

    def products(r):
        rs = slice(r * rc, (r + 1) * rc)
        out = []
        for k, (a, wb) in enumerate(zip(a_refs, w_bf)):
            acc = jnp.dot(a[rs, :], wb[...], preferred_element_type=F32)
            if park:
                slot = (i + r) % 2
                parked[k][slot] = acc
                acc = parked[k][slot]
            out.append(acc)
        return out

    nxt = products(0)
    for r in range(tm // rc):
        accs = nxt
        if r + 1 < tm // rc:
            nxt = products(r + 1)
        epilogue(accs, extra, outs, scratch, i, slice(r * rc, (r + 1) * rc), carry)


def _mm(name, a_list, w_list, extras, epilogue, outs, *, tm, tn, n_cols, scratch=(), rc=128, park=False):
    M = a_list[0].shape[0]
    rc = min(rc, tm)
    in_specs = []
    for a in a_list:
        in_specs.append(pl.BlockSpec((tm, a.shape[1]), lambda j, i: (i, 0)))
    stage = []
    scratch_shapes = []
    for w, start, t in w_list:
        if t:
            k_dim = w.shape[1]
            in_specs.append(pl.BlockSpec((pl.Element(tn), pl.Element(k_dim)),
                                         functools.partial(
                                             lambda j, i, s: (pl.multiple_of(s + j * tn, SUBLANES), 0), s=start)))
        else:
            k_dim = w.shape[0]
            in_specs.append(pl.BlockSpec((k_dim, tn), functools.partial(lambda j, i, s: (0, j + s), s=start)))
        stage.append(t or w.dtype != BF16)
        if stage[-1]:
            scratch_shapes.append(pltpu.VMEM((k_dim, tn), BF16))
    for _, bs, im in extras:
        in_specs.append(pl.BlockSpec(bs, im))
    out_specs = [pl.BlockSpec(bs, im) for _, _, bs, im in outs]
    out_shape = [jax.ShapeDtypeStruct(s, d) for s, d, _, _ in outs]
    return pl.pallas_call(
        functools.partial(_mm_kernel, n_a=len(a_list), n_mm=len(w_list), n_extra=len(extras), n_out=len(outs),
                          stage=tuple(stage), trans=tuple(t for _, _, t in w_list), rc=rc, epilogue=epilogue, park=park),
        grid=(n_cols // tn, M // tm),
        in_specs=in_specs,
        out_specs=out_specs,
        out_shape=out_shape,
        scratch_shapes=(scratch_shapes + list(scratch)
                        + ([pltpu.VMEM((2, rc, tn), F32) for _ in w_list] if park else [])),
        compiler_params=_params(("arbitrary", "arbitrary")),
        name=name,
    )(*a_list, *[w[0] for w in w_list], *[e[0] for e in extras])


def _tile_map(j, i):
    return (i, j)


def _col_map(j, i):
    return (0, j)


def _fixed_map(j, i):
    return (0, 0)


def _ep_store(accs, extra, outs, scratch, i, rs, carry):
    outs[0][rs, :] = accs[0].astype(outs[0].dtype)


def _ep_silu(accs, extra, outs, scratch, i, rs, carry):
    outs[0][rs, :] = _silu(accs[0]).astype(outs[0].dtype)


def _ep_sigmoid_bias(accs, extra, outs, scratch, i, rs, carry):
    outs[0][rs, :] = _sigmoid(accs[0] + extra[0][...]).astype(outs[0].dtype)


def _ep_head_norm(accs, extra, outs, scratch, i, rs, carry, *, scale):
    acc = accs[0]
    w = extra[0][...] * scale
    for c in range(acc.shape[1] // SB_HEAD_DIM):
        g = acc[:, c * SB_HEAD_DIM:(c + 1) * SB_HEAD_DIM]
        ms = jnp.mean(g * g, axis=-1, keepdims=True)
        outs[0][rs, c * SB_HEAD_DIM:(c + 1) * SB_HEAD_DIM] = (g * lax.rsqrt(ms + EPS) * w).astype(outs[0].dtype)


def _causal_conv(acc, cw_ref, cb_ref, tail_ref, i, rs, carry, tm, tiles_per_seq):
    taps = cw_ref.shape[0]
    if rs.start == 0:
        tail = jnp.where(i % tiles_per_seq == 0, 0.0, tail_ref[...])
    else:
        tail = carry["tail"]
    hist = jnp.concatenate([tail, acc], axis=0)
    y = cb_ref[...]
    for k in range(taps - 1):
        y = y + cw_ref[k:k + 1, :] * pltpu.roll(hist, taps - 1 - k, 0)[SUBLANES:, :]
    y = y + cw_ref[taps - 1:taps, :] * acc
    carry["tail"] = acc[acc.shape[0] - SUBLANES:, :]
    if rs.stop == tm:
        tail_ref[...] = carry["tail"]
    return y


def _ep_conv_silu(accs, extra, outs, scratch, i, rs, carry, *, tiles_per_seq):
    y = _causal_conv(accs[0], extra[0], extra[1], scratch[0], i, rs, carry, outs[0].shape[0], tiles_per_seq)
    outs[0][rs, :] = _silu(y).astype(outs[0].dtype)


def _ep_ffn_up(accs, extra, outs, scratch, i, rs, carry, *, tiles_per_seq):
    y = _causal_conv(accs[0], extra[0], extra[1], scratch[0], i, rs, carry, outs[0].shape[0], tiles_per_seq)
    outs[0][rs, :] = (_silu(y) * accs[1]).astype(outs[0].dtype)


def _ep_mix(accs, extra, outs, scratch, i, rs, carry):
    g0 = extra[0][rs, :].astype(F32)
    g1 = extra[1][rs, :].astype(F32)
    outs[0][rs, :] = (g0 * accs[0] + g1 * accs[1]).astype(outs[0].dtype)


def _ep_residual_norm(accs, extra, outs, scratch, i, rs, carry):
    x1 = extra[0][rs, :] + accs[0]
    outs[0][rs, :] = x1
    ms = jnp.mean(x1 * x1, axis=-1, keepdims=True)
    outs[1][rs, :] = (x1 * lax.rsqrt(ms + EPS) * extra[1][...]).astype(outs[1].dtype)


def _ep_residual(accs, extra, outs, scratch, i, rs, carry):
    outs[0][rs, :] = extra[0][rs, :] + accs[0]


def _split3(x):
    h = x.astype(BF16)
    r = x - h.astype(F32)
    m = r.astype(BF16)
    l = (r - m.astype(F32)).astype(BF16)
    return h, m, l


def _ssd_kernel(xbc_ref, dt_ref, zs_ref, alog_ref, dskip_ref, nw_ref, tri_ref,
                o_ref, st, ybuf, *, d_inner, n_groups):
    Q = CHUNK
    P = SSM_HEAD_DIM
    NS = SSM_STATE
    gw = d_inner // n_groups
    pairs = gw // (2 * P)
    c = pl.program_id(1)

    @pl.when(c == 0)
    def _():
        st[...] = jnp.zeros(st.shape, F32)

    dt = dt_ref[...]
    a = dt * (-jnp.exp(alog_ref[...]) * math.log2(math.e))
    tri = tri_ref[...]
    ah, am, al = _split3(a)
    acs = (jnp.dot(tri, ah, preferred_element_type=F32)
           + jnp.dot(tri, am, preferred_element_type=F32)
           + jnp.dot(tri, al, preferred_element_type=F32))
    src_t = (acs - jnp.log2(dt)).T
    wdec_t = (dt * jnp.exp2(acs[Q - 1:Q, :] - acs)).T
    qi = lax.broadcasted_iota(jnp.int32, (Q, Q), 0)
    si = lax.broadcasted_iota(jnp.int32, (Q, Q), 1)
    causal = qi >= si
    low = si < P
    low_b = jnp.where(low[0:1, :], 1.0, 0.0).astype(BF16)
    high_b = jnp.where(low[0:1, :], 0.0, 1.0).astype(BF16)

    for g in range(n_groups):
        b_g = xbc_ref[:, d_inner + g * NS:d_inner + (g + 1) * NS]
        c_g = xbc_ref[:, d_inner + (n_groups + g) * NS:d_inner + (n_groups + g + 1) * NS]
        cb = lax.dot_general(c_g, b_g, (((1,), (1,)), ((), ())), preferred_element_type=F32)
        b_t = b_g.astype(F32).T
        gl = slice(g * gw, (g + 1) * gw)
        y_off = jnp.dot(c_g, st[:, gl].astype(BF16), preferred_element_type=F32)
        for p in range(pairs):
            heads = (g * 2 * pairs + 2 * p, g * 2 * pairs + 2 * p + 1)
            pl_ = slice(g * gw + p * 2 * P, g * gw + (p + 1) * 2 * P)
            x_pair = xbc_ref[:, pl_]
            colb = [jnp.broadcast_to(acs[:, h:h + 1], (Q, Q)) for h in heads]
            ms = [(cb * jnp.exp2(jnp.where(causal, cbh - src_t[h:h + 1, :], -jnp.inf))).astype(BF16)
                  for h, cbh in zip(heads, colb)]
            bw = [(b_t * wdec_t[h:h + 1, :]).astype(BF16) for h in heads]
            lhs = jnp.concatenate([jnp.concatenate(ms, axis=1), jnp.concatenate(bw, axis=1)], axis=0)
            x_bd = jnp.concatenate([x_pair * low_b, x_pair * high_b], axis=0)
            res = jnp.dot(lhs, x_bd, preferred_element_type=F32)
            e_pair = jnp.exp2(jnp.where(low, colb[0], colb[1]))
            ybuf[:, pl_] = (res[:Q] + e_pair * y_off[:, p * 2 * P:(p + 1) * 2 * P]
                            + dskip_ref[:, pl_] * x_pair.astype(F32))
            st[:, pl_] = st[:, pl_] * e_pair[Q - 1:Q, :] + res[Q:]

    for g in range(n_groups):
        gl = slice(g * gw, (g + 1) * gw)
        yg = ybuf[:, gl] * zs_ref[:, gl].astype(F32)
        ms = jnp.mean(yg * yg, axis=-1, keepdims=True)
        o_ref[:, gl] = (yg * lax.rsqrt(ms + EPS) * nw_ref[:, gl]).astype(o_ref.dtype)


def _ssd(xbc, dt, zs, a_log_pad, dskip_row, norm_w, batch, seq):
    M, conv_dim = xbc.shape
    d_inner = zs.shape[1]
    n_groups = SSM_GROUPS
    nc = seq // CHUNK
    tri = (jnp.arange(CHUNK)[:, None] >= jnp.arange(CHUNK)[None, :]).astype(BF16)
    row = lambda b, c: (b * nc + c, 0)
    fixed = lambda b, c: (0, 0)
    return pl.pallas_call(
        functools.partial(_ssd_kernel, d_inner=d_inner, n_groups=n_groups),
        grid=(batch, nc),
        in_specs=[pl.BlockSpec((CHUNK, conv_dim), row),
                  pl.BlockSpec((CHUNK, LANES), row),
                  pl.BlockSpec((CHUNK, d_inner), row),
                  pl.BlockSpec((1, LANES), fixed),
                  pl.BlockSpec((1, d_inner), fixed),
                  pl.BlockSpec((1, d_inner), fixed),
                  pl.BlockSpec((CHUNK, CHUNK), fixed)],
        out_specs=pl.BlockSpec((CHUNK, d_inner), row),
        out_shape=jax.ShapeDtypeStruct((M, d_inner), BF16),
        scratch_shapes=[pltpu.VMEM((SSM_STATE, d_inner), F32),
                        pltpu.VMEM((CHUNK, d_inner), F32)],
        compiler_params=_params(("arbitrary", "arbitrary")),
        name="ssd",
    )(xbc, dt, zs, a_log_pad, dskip_row, norm_w.reshape(1, d_inner), tri)


def _attn_kernel(q_ref, k_ref, v_ref, tri_ref, o_ref, *, tq):
    i = pl.program_id(2)
    hb = q_ref.shape[1] // SB_HEAD_DIM
    tri = tri_ref[...]
    rows = lax.broadcasted_iota(jnp.int32, (tq, tq), 0)
    cols = lax.broadcasted_iota(jnp.int32, (tq, tq), 1)
    mask = cols < rows

    def block(h, kb, run, diagonal):
        ks = pl.multiple_of(kb * tq, tq)
        hs = slice(h * SB_HEAD_DIM, (h + 1) * SB_HEAD_DIM)
        kblk = k_ref[pl.ds(ks, tq), hs]
        vblk = v_ref[pl.ds(ks, tq), hs]
        z = lax.dot_general(q_ref[:, hs], kblk, (((1,), (1,)), ((), ())), preferred_element_type=F32)
        sp = jnp.maximum(z, 0.0) + jnp.log2(1.0 + jnp.exp2(-jnp.abs(z)))
        nlk = jnp.where(mask, sp, 0.0) if diagonal else sp
        later = jnp.dot(nlk.astype(BF16), tri, preferred_element_type=F32)
        att = jnp.exp2(((z - sp) - later) - run)
        if diagonal:
            att = jnp.where(mask, att, 0.0)
        pv = jnp.dot(att.astype(BF16), vblk, preferred_element_type=F32)
        return pv, run + jnp.sum(nlk, axis=-1, keepdims=True)

    def live(runs):
        return jnp.min(functools.reduce(jnp.minimum, runs)) < F32_EXP2_UNDERFLOW

    has_prev = i > 0
    chains = [(h, kb, diag) for h in range(hb) for kb, diag in ((i, True), (jnp.maximum(i - 1, 0), False))]
    hsl = lambda h: slice(h * SB_HEAD_DIM, (h + 1) * SB_HEAD_DIM)
    rowsl = lambda kb: pl.ds(pl.multiple_of(kb * tq, tq), tq)
    zs = [lax.dot_general(q_ref[:, hsl(h)], k_ref[rowsl(kb), hsl(h)], (((1,), (1,)), ((), ())),
                          preferred_element_type=F32) for h, kb, _ in chains]
    sps, sums, laters = [], [], []
    for z, (_, _, diag) in zip(zs, chains):
        sp = jnp.maximum(z, 0.0) + jnp.log2(1.0 + jnp.exp2(-jnp.abs(z)))
        nlk = jnp.where(mask, sp, 0.0) if diag else sp
        sps.append(sp)
        sums.append(jnp.sum(nlk, axis=-1, keepdims=True))
        laters.append(jnp.dot(nlk.astype(BF16), tri, preferred_element_type=F32))
    pvs = []
    for c, (h, kb, diag) in enumerate(chains):
        e = (zs[c] - sps[c]) - laters[c]
        att = jnp.where(mask, jnp.exp2(e), 0.0) if diag else jnp.exp2(e - sums[c - 1])
        pvs.append(jnp.dot(att.astype(BF16), v_ref[rowsl(kb), hsl(h)], preferred_element_type=F32))
    accs = [pvs[2 * h] + jnp.where(has_prev, pvs[2 * h + 1], 0.0) for h in range(hb)]
    runs = [jnp.where(has_prev, sums[2 * h] + sums[2 * h + 1], sums[2 * h]) for h in range(hb)]

    def cond(c):
        return jnp.logical_and(c[0] >= 0, c[1])

    def body(c):
        kb, _, accs, runs = c
        new = [block(h, kb, runs[h], False) for h in range(hb)]
        accs = tuple(a + pv for a, (pv, _) in zip(accs, new))
        runs = tuple(r for _, r in new)
        return kb - 1, live(runs), accs, runs

    _, _, accs, _ = lax.while_loop(cond, body, (i - 2, live(runs), tuple(accs), tuple(runs)))
    for h in range(hb):
        o_ref[:, h * SB_HEAD_DIM:(h + 1) * SB_HEAD_DIM] = accs[h].astype(o_ref.dtype)


def _attention(q, k, v, batch, seq, *, tq, heads_per_step=4):
    M, W = q.shape
    hw = heads_per_step * SB_HEAD_DIM
    nq = seq // tq
    tri = (jnp.arange(tq)[:, None] > jnp.arange(tq)[None, :]).astype(BF16)
    return pl.pallas_call(
        functools.partial(_attn_kernel, tq=tq),
        grid=(batch, W // hw, nq),
        in_specs=[pl.BlockSpec((tq, hw), lambda b, h, i: (b * nq + i, h)),
                  pl.BlockSpec((seq, hw), lambda b, h, i: (b, h)),
                  pl.BlockSpec((seq, hw), lambda b, h, i: (b, h)),
                  pl.BlockSpec((tq, tq), lambda b, h, i: (0, 0))],
        out_specs=pl.BlockSpec((tq, hw), lambda b, h, i: (b * nq + i, h)),
        out_shape=jax.ShapeDtypeStruct((M, W), BF16),
        compiler_params=_params(("arbitrary", "arbitrary", "arbitrary")),
        name="stickbreak_attn",
    )(q, k, v, tri)


def _tile(n, pref):
    t = pref
    while n % t:
        t //= 2
    return t


def _layer(x2d, batch, seq, norm1_w, w_in, conv_ssm_w, conv_ssm_b, dt_bias, a_log, d_skip, ssm_norm_w,
           q_norm_w, k_norm_w, gate_b, w_ssm_out, w_att_out, w_o, norm2_w, w_up, conv_ffn_w,
           conv_ffn_b, w_down):
    M, D = x2d.shape
    d_inner = w_ssm_out.shape[0]
    ssm_heads = dt_bias.shape[0]
    conv_dim = conv_ssm_w.shape[1]
    sb_width = w_att_out.shape[0]
    ffn = w_down.shape[0]
    tm = _tile(seq, 1024)
    tn = 1024

    o_dt = d_inner + conv_dim
    o_q = o_dt + ssm_heads
    assert d_inner % tn == 0 and conv_dim % tn == 0 and sb_width % tn == 0 and o_q % SUBLANES == 0
    w_in_t = jnp.swapaxes(w_in, 0, 1)
    w_dt = jnp.pad(w_in_t[o_dt:o_q], ((0, LANES - ssm_heads), (0, 0))).astype(BF16)
    pad_heads = lambda p: jnp.pad(p, (0, LANES - ssm_heads)).reshape(1, LANES)

    u, dt = _rmsnorm_dt(x2d, norm1_w, w_dt, pad_heads(dt_bias), tm)

    tm_big = _tile(seq, 2048)

    def proj(name, w, start, n, epilogue, extras, dtype, tn=tn, tm=tm_big, scratch=(), park=False):
        return _mm(name, [u], [(w, start, True)], extras, epilogue, [((M, n), dtype, (tm, tn), _tile_map)],
                   tm=tm, tn=tn, n_cols=n, scratch=scratch, park=park)[0]

    zs = proj("proj_z", w_in_t, 0, d_inner, _ep_silu, [], BF16)
    tn_c = 512
    xbc = proj("proj_xbc", w_in_t, d_inner, conv_dim,
               functools.partial(_ep_conv_silu, tiles_per_seq=seq // tm_big),
               [(conv_ssm_w, (SSM_CONV, tn_c), _col_map), (conv_ssm_b.reshape(1, -1), (1, tn_c), _col_map)],
               BF16, tn=tn_c, scratch=[pltpu.VMEM((SUBLANES, tn_c), F32)], park=True)
    scale = math.log2(math.e) / math.sqrt(SB_HEAD_DIM)
    head_w = lambda w: (w.reshape(1, SB_HEAD_DIM), (1, SB_HEAD_DIM), _fixed_map)
    q = proj("proj_q", w_in_t, o_q, sb_width, functools.partial(_ep_head_norm, scale=scale),
             [head_w(q_norm_w)], BF16)
    k = proj("proj_k", w_in_t, o_q + sb_width, sb_width, functools.partial(_ep_head_norm, scale=1.0),
             [head_w(k_norm_w)], BF16)
    v = proj("proj_v", w_in_t, o_q + 2 * sb_width, sb_width, _ep_store, [], BF16)
    gates = proj("proj_gate", w_in_t, o_q + 3 * sb_width, 2 * D, _ep_sigmoid_bias,
                 [(gate_b.reshape(1, -1), (1, tn), _col_map)], BF16)

    y_ssm_n = _ssd(xbc, dt, zs, pad_heads(a_log),
                   jnp.repeat(d_skip, SSM_HEAD_DIM).reshape(1, d_inner), ssm_norm_w, batch, seq)
    att = _attention(q, k, v, batch, seq, tq=_tile(seq, 256))

    tn_s = 512
    tm_s = _tile(seq, 512)
    nbm = D // tn_s
    mixed = _mm("mix", [y_ssm_n, att], [(w_ssm_out, 0, False), (w_att_out, 0, False)],
                [(gates, (tm_s, tn_s), _tile_map), (gates, (tm_s, tn_s), lambda j, i: (i, j + nbm))],
                _ep_mix, [((M, D), BF16, (tm_s, tn_s), _tile_map)], tm=tm_s, tn=tn_s, n_cols=D)[0]

    x1, u2 = _mm("out_proj", [mixed], [(w_o.astype(BF16), 0, False)],
                 [(x2d, (tm_s, D), _tile_map), (norm2_w.reshape(1, D), (1, D), _col_map)],
                 _ep_residual_norm,
                 [((M, D), F32, (tm_s, D), _tile_map), ((M, D), BF16, (tm_s, D), _tile_map)],
                 tm=tm_s, tn=D, n_cols=D)

    hdn = _mm("ffn_up", [u2], [(w_up, 0, False), (w_up, ffn // tn_s, False)],
              [(conv_ffn_w, (FFN_CONV, tn_s), _col_map), (conv_ffn_b.reshape(1, -1), (1, tn_s), _col_map)],
              functools.partial(_ep_ffn_up, tiles_per_seq=seq // tm_big),
              [((M, ffn), BF16, (tm_big, tn_s), _tile_map)], tm=tm_big, tn=tn_s, n_cols=ffn,
              scratch=[pltpu.VMEM((SUBLANES, tn_s), F32)])[0]
    out = _mm("ffn_down", [hdn], [(w_down, 0, False)], [(x1, (tm_s, tn_s), _tile_map)], _ep_residual,
              [((M, D), F32, (tm_s, tn_s), _tile_map)], tm=tm_s, tn=tn_s, n_cols=D)[0]
    return out


def kernel(x, norm1_w, w_in, conv_ssm_w, conv_ssm_b, dt_bias, a_log, d_skip, ssm_norm_w, q_norm_w,
           k_norm_w, gate_b, w_ssm_out, w_att_out, w_o, norm2_w, w_up, conv_ffn_w, conv_ffn_b, w_down):
    b, s, d = x.shape
    x2d = x.reshape(b * s, d)
    for l in range(norm1_w.shape[0]):
        x2d = _layer(x2d, b, s, norm1_w[l], w_in[l], conv_ssm_w[l], conv_ssm_b[l], dt_bias[l], a_log[l],
                     d_skip[l], ssm_norm_w[l], q_norm_w[l], k_norm_w[l], gate_b[l], w_ssm_out[l],
                     w_att_out[l], w_o[l], norm2_w[l], w_up[l], conv_ffn_w[l], conv_ffn_b[l], w_down[l])
    return x2d.reshape(b, s, d)
```

```python
import functools
import math

import jax
import jax.numpy as jnp
from jax import lax
from jax.experimental import pallas as pl
from jax.experimental.pallas import tpu as pltpu

EPS = 1e-6
F32 = jnp.float32
BF16 = jnp.bfloat16

SSM_HEAD_DIM = 64
SSM_GROUPS = 8
SSM_STATE = 128
SSM_CONV = 4
CHUNK = 128
SB_HEAD_DIM = 128
FFN_CONV = 3

LANES = 128
SUBLANES = 8
VMEM_LIMIT_BYTES = 56 * 1024 * 1024

F32_EXP2_UNDERFLOW = 150.0


def _sigmoid(x):
    return 1.0 / (1.0 + jnp.exp(-x))


def _silu(x):
    h = 0.5 * x
    return h + h * jnp.tanh(h)


def _softplus(x):
    return jnp.maximum(x, 0.0) + jnp.log1p(jnp.exp(-jnp.abs(x)))


def _params(sem):
    return pltpu.CompilerParams(dimension_semantics=sem, vmem_limit_bytes=VMEM_LIMIT_BYTES)


def _rmsnorm_dt_kernel(x_ref, w_ref, wdt_ref, bdt_ref, o_ref, dt_ref):
    x = x_ref[...]
    ms = jnp.mean(x * x, axis=-1, keepdims=True)
    u = (x * lax.rsqrt(ms + EPS) * w_ref[...]).astype(o_ref.dtype)
    o_ref[...] = u
    acc = lax.dot_general(u, wdt_ref[...], (((1,), (1,)), ((), ())), preferred_element_type=F32)
    dt_ref[...] = _softplus(acc + bdt_ref[...])


def _rmsnorm_dt(x2d, w, w_dt, b_dt, tm):
    M, D = x2d.shape
    return pl.pallas_call(
        _rmsnorm_dt_kernel,
        grid=(M // tm,),
        in_specs=[pl.BlockSpec((tm, D), lambda i: (i, 0)),
                  pl.BlockSpec((1, D), lambda i: (0, 0)),
                  pl.BlockSpec((LANES, D), lambda i: (0, 0)),
                  pl.BlockSpec((1, LANES), lambda i: (0, 0))],
        out_specs=[pl.BlockSpec((tm, D), lambda i: (i, 0)),
                   pl.BlockSpec((tm, LANES), lambda i: (i, 0))],
        out_shape=[jax.ShapeDtypeStruct((M, D), BF16), jax.ShapeDtypeStruct((M, LANES), F32)],
        compiler_params=_params(("arbitrary",)),
        name="rmsnorm_dt",
    )(x2d, w.reshape(1, D), w_dt, b_dt)


def _mm_kernel(*refs, n_a, n_mm, n_extra, n_out, stage, trans, rc, epilogue, park):
    a_refs = [refs[k % n_a] for k in range(n_mm)]
    refs = refs[n_a:]
    w_refs = refs[:n_mm]
    extra = refs[n_mm:n_mm + n_extra]
    outs = refs[n_mm + n_extra:n_mm + n_extra + n_out]
    scratch = list(refs[n_mm + n_extra + n_out:])
    i = pl.program_id(1)
    w_bf = [scratch.pop(0) if s else w for w, s in zip(w_refs, stage)]

    @pl.when(i == 0)
    def _():
        for w, wb, s, t in zip(w_refs, w_bf, stage, trans):
            if s:
                wv = w[...].astype(BF16)
                wb[...] = wv.T if t else wv

    tm = a_refs[0].shape[0]
    carry = {}
    parked = [scratch.pop() for _ in range(n_mm)][::-1] if park else None

    def products(r):
        rs = slice(r * rc, (r + 1) * rc)
        out = []
        for k, (a, wb) in enumerate(zip(a_refs, w_bf)):
            acc = jnp.dot(a[rs, :], wb[...], preferred_element_type=F32)
            if park:
                slot = (i + r) % 2
                parked[k][slot] = acc
                acc = parked[k][slot]
            out.append(acc)
        return out

    nxt = products(0)
    for r in range(tm // rc):
        accs = nxt
        if r + 1 < tm // rc:
            nxt = products(r + 1)
        epilogue(accs, extra, outs, scratch, i, slice(r * rc, (r + 1) * rc), carry)


def _mm(name, a_list, w_list, extras, epilogue, outs, *, tm, tn, n_cols, scratch=(), rc=128, park=False):
    M = a_list[0].shape[0]
    rc = min(rc, tm)
    in_specs = []
    for a in a_list:
        in_specs.append(pl.BlockSpec((tm, a.shape[1]), lambda j, i: (i, 0)))
    stage = []
    scratch_shapes = []
    for w, start, t in w_list:
        if t:
            k_dim = w.shape[1]
            in_specs.append(pl.BlockSpec((pl.Element(tn), pl.Element(k_dim)),
                                         functools.partial(
                                             lambda j, i, s: (pl.multiple_of(s + j * tn, SUBLANES), 0), s=start)))
        else:
            k_dim = w.shape[0]
            in_specs.append(pl.BlockSpec((k_dim, tn), functools.partial(lambda j, i, s: (0, j + s), s=start)))
        stage.append(t or w.dtype != BF16)
        if stage[-1]:
            scratch_shapes.append(pltpu.VMEM((k_dim, tn), BF16))
    for _, bs, im in extras:
        in_specs.append(pl.BlockSpec(bs, im))
    out_specs = [pl.BlockSpec(bs, im) for _, _, bs, im in outs]
    out_shape = [jax.ShapeDtypeStruct(s, d) for s, d, _, _ in outs]
    return pl.pallas_call(
        functools.partial(_mm_kernel, n_a=len(a_list), n_mm=len(w_list), n_extra=len(extras), n_out=len(outs),
                          stage=tuple(stage), trans=tuple(t for _, _, t in w_list), rc=rc, epilogue=epilogue, park=park),
        grid=(n_cols // tn, M // tm),
        in_specs=in_specs,
        out_specs=out_specs,
        out_shape=out_shape,
        scratch_shapes=(scratch_shapes + list(scratch)
                        + ([pltpu.VMEM((2, rc, tn), F32) for _ in w_list] if park else [])),
        compiler_params=_params(("arbitrary", "arbitrary")),
        name=name,
    )(*a_list, *[w[0] for w in w_list], *[e[0] for e in extras])


def _tile_map(j, i):
    return (i, j)


def _col_map(j, i):
    return (0, j)


def _fixed_map(j, i):
    return (0, 0)


def _ep_store(accs, extra, outs, scratch, i, rs, carry):
    outs[0][rs, :] = accs[0].astype(outs[0].dtype)


def _ep_silu(accs, extra, outs, scratch, i, rs, carry):
    outs[0][rs, :] = _silu(accs[0]).astype(outs[0].dtype)


def _ep_sigmoid_bias(accs, extra, outs, scratch, i, rs, carry):
    outs[0][rs, :] = _sigmoid(accs[0] + extra[0][...]).astype(outs[0].dtype)


def _ep_head_norm(accs, extra, outs, scratch, i, rs, carry, *, scale):
    acc = accs[0]
    w = extra[0][...] * scale
    for c in range(acc.shape[1] // SB_HEAD_DIM):
        g = acc[:, c * SB_HEAD_DIM:(c + 1) * SB_HEAD_DIM]
        ms = jnp.mean(g * g, axis=-1, keepdims=True)
        outs[0][rs, c * SB_HEAD_DIM:(c + 1) * SB_HEAD_DIM] = (g * lax.rsqrt(ms + EPS) * w).astype(outs[0].dtype)


def _causal_conv(acc, cw_ref, cb_ref, tail_ref, i, rs, carry, tm, tiles_per_seq):
    taps = cw_ref.shape[0]
    if rs.start == 0:
        tail = jnp.where(i % tiles_per_seq == 0, 0.0, tail_ref[...])
    else:
        tail = carry["tail"]
    hist = jnp.concatenate([tail, acc], axis=0)
    y = cb_ref[...]
    for k in range(taps - 1):
        y = y + cw_ref[k:k + 1, :] * pltpu.roll(hist, taps - 1 - k, 0)[SUBLANES:, :]
    y = y + cw_ref[taps - 1:taps, :] * acc
    carry["tail"] = acc[acc.shape[0] - SUBLANES:, :]
    if rs.stop == tm:
        tail_ref[...] = carry["tail"]
    return y


def _ep_conv_silu(accs, extra, outs, scratch, i, rs, carry, *, tiles_per_seq):
    y = _causal_conv(accs[0], extra[0], extra[1], scratch[0], i, rs, carry, outs[0].shape[0], tiles_per_seq)
    outs[0][rs, :] = _silu(y).astype(outs[0].dtype)


def _ep_ffn_up(accs, extra, outs, scratch, i, rs, carry, *, tiles_per_seq):
    y = _causal_conv(accs[0], extra[0], extra[1], scratch[0], i, rs, carry, outs[0].shape[0], tiles_per_seq)
    outs[0][rs, :] = (_silu(y) * accs[1]).astype(outs[0].dtype)


def _ep_mix(accs, extra, outs, scratch, i, rs, carry):
    g0 = extra[0][rs, :].astype(F32)
    g1 = extra[1][rs, :].astype(F32)
    outs[0][rs, :] = (g0 * accs[0] + g1 * accs[1]).astype(outs[0].dtype)


def _ep_residual_norm(accs, extra, outs, scratch, i, rs, carry):
    x1 = extra[0][rs, :] + accs[0]
    outs[0][rs, :] = x1
    ms = jnp.mean(x1 * x1, axis=-1, keepdims=True)
    outs[1][rs, :] = (x1 * lax.rsqrt(ms + EPS) * extra[1][...]).astype(outs[1].dtype)


def _ep_residual(accs, extra, outs, scratch, i, rs, carry):
    outs[0][rs, :] = extra[0][rs, :] + accs[0]


def _split3(x):
    h = x.astype(BF16)
    r = x - h.astype(F32)
    m = r.astype(BF16)
    l = (r - m.astype(F32)).astype(BF16)
    return h, m, l


def _ssd_kernel(xbc_ref, dt_ref, zs_ref, alog_ref, dskip_ref, nw_ref, tri_ref,
                o_ref, st, ybuf, *, d_inner, n_groups):
    Q = CHUNK
    P = SSM_HEAD_DIM
    NS = SSM_STATE
    gw = d_inner // n_groups
    pairs = gw // (2 * P)
    c = pl.program_id(1)

    @pl.when(c == 0)
    def _():
        st[...] = jnp.zeros(st.shape, F32)

    dt = dt_ref[...]
    a = dt * (-jnp.exp(alog_ref[...]) * math.log2(math.e))
    tri = tri_ref[...]
    ah, am, al = _split3(a)
    acs = (jnp.dot(tri, ah, preferred_element_type=F32)
           + jnp.dot(tri, am, preferred_element_type=F32)
           + jnp.dot(tri, al, preferred_element_type=F32))
    src_t = (acs - jnp.log2(dt)).T
    wdec_t = (dt * jnp.exp2(acs[Q - 1:Q, :] - acs)).T
    qi = lax.broadcasted_iota(jnp.int32, (Q, Q), 0)
    si = lax.broadcasted_iota(jnp.int32, (Q, Q), 1)
    causal = qi >= si
    low = si < P
    low_b = jnp.where(low[0:1, :], 1.0, 0.0).astype(BF16)
    high_b = jnp.where(low[0:1, :], 0.0, 1.0).astype(BF16)

    for g in range(n_groups):
        b_g = xbc_ref[:, d_inner + g * NS:d_inner + (g + 1) * NS]
        c_g = xbc_ref[:, d_inner + (n_groups + g) * NS:d_inner + (n_groups + g + 1) * NS]
        cb = lax.dot_general(c_g, b_g, (((1,), (1,)), ((), ())), preferred_element_type=F32)
        b_t = b_g.astype(F32).T
        gl = slice(g * gw, (g + 1) * gw)
        for p in range(pairs):
            heads = (g * 2 * pairs + 2 * p, g * 2 * pairs + 2 * p + 1)
            pl_ = slice(g * gw + p * 2 * P, g * gw + (p + 1) * 2 * P)
            x_pair = xbc_ref[:, pl_]
            colb = [jnp.broadcast_to(acs[:, h:h + 1], (Q, Q)) for h in heads]
            ms = [(cb * jnp.exp2(jnp.where(causal, cbh - src_t[h:h + 1, :], -jnp.inf))).astype(BF16)
                  for h, cbh in zip(heads, colb)]
            bw = [(b_t * wdec_t[h:h + 1, :]).astype(BF16) for h in heads]
            lhs = jnp.concatenate([jnp.concatenate(ms, axis=1), jnp.concatenate(bw, axis=1)], axis=0)
            x_bd = jnp.concatenate([x_pair * low_b, x_pair * high_b], axis=0)
            res = jnp.dot(lhs, x_bd, preferred_element_type=F32)
            e_pair = jnp.exp2(jnp.where(low, colb[0], colb[1]))
            y_off = jnp.dot(c_g, st[:, pl_].astype(BF16), preferred_element_type=F32)
            ybuf[:, pl_] = (res[:Q] + e_pair * y_off
                            + dskip_ref[:, pl_] * x_pair.astype(F32))
            st[:, pl_] = st[:, pl_] * e_pair[Q - 1:Q, :] + res[Q:]

    for g in range(n_groups):
        gl = slice(g * gw, (g + 1) * gw)
        yg = ybuf[:, gl] * zs_ref[:, gl].astype(F32)
        ms = jnp.mean(yg * yg, axis=-1, keepdims=True)
        o_ref[:, gl] = (yg * lax.rsqrt(ms + EPS) * nw_ref[:, gl]).astype(o_ref.dtype)


def _ssd(xbc, dt, zs, a_log_pad, dskip_row, norm_w, batch, seq):
    M, conv_dim = xbc.shape
    d_inner = zs.shape[1]
    n_groups = SSM_GROUPS
    nc = seq // CHUNK
    tri = (jnp.arange(CHUNK)[:, None] >= jnp.arange(CHUNK)[None, :]).astype(BF16)
    row = lambda b, c: (b * nc + c, 0)
    fixed = lambda b, c: (0, 0)
    return pl.pallas_call(
        functools.partial(_ssd_kernel, d_inner=d_inner, n_groups=n_groups),
        grid=(batch, nc),
        in_specs=[pl.BlockSpec((CHUNK, conv_dim), row),
                  pl.BlockSpec((CHUNK, LANES), row),
                  pl.BlockSpec((CHUNK, d_inner), row),
                  pl.BlockSpec((1, LANES), fixed),
                  pl.BlockSpec((1, d_inner), fixed),
                  pl.BlockSpec((1, d_inner), fixed),
                  pl.BlockSpec((CHUNK, CHUNK), fixed)],
        out_specs=pl.BlockSpec((CHUNK, d_inner), row),
        out_shape=jax.ShapeDtypeStruct((M, d_inner), BF16),
        scratch_shapes=[pltpu.VMEM((SSM_STATE, d_inner), F32),
                        pltpu.VMEM((CHUNK, d_inner), F32)],
        compiler_params=_params(("arbitrary", "arbitrary")),
        name="ssd",
    )(xbc, dt, zs, a_log_pad, dskip_row, norm_w.reshape(1, d_inner), tri)


def _attn_kernel(q_ref, k_ref, v_ref, tri_ref, o_ref, *, tq):
    i = pl.program_id(2)
    hb = q_ref.shape[1] // SB_HEAD_DIM
    tri = tri_ref[...]
    rows = lax.broadcasted_iota(jnp.int32, (tq, tq), 0)
    cols = lax.broadcasted_iota(jnp.int32, (tq, tq), 1)
    mask = cols < rows

    def block(h, kb, run, diagonal):
        ks = pl.multiple_of(kb * tq, tq)
        hs = slice(h * SB_HEAD_DIM, (h + 1) * SB_HEAD_DIM)
        kblk = k_ref[pl.ds(ks, tq), hs]
        vblk = v_ref[pl.ds(ks, tq), hs]
        z = lax.dot_general(q_ref[:, hs], kblk, (((1,), (1,)), ((), ())), preferred_element_type=F32)
        sp = jnp.maximum(z, 0.0) + jnp.log2(1.0 + jnp.exp2(-jnp.abs(z)))
        nlk = jnp.where(mask, sp, 0.0) if diagonal else sp
        later = jnp.dot(nlk.astype(BF16), tri, preferred_element_type=F32)
        att = jnp.exp2(((z - sp) - later) - run)
        if diagonal:
            att = jnp.where(mask, att, 0.0)
        pv = jnp.dot(att.astype(BF16), vblk, preferred_element_type=F32)
        return pv, run + jnp.sum(nlk, axis=-1, keepdims=True)

    def live(runs):
        return jnp.min(functools.reduce(jnp.minimum, runs)) < F32_EXP2_UNDERFLOW

    has_prev = i > 0
    chains = [(h, kb, diag) for h in range(hb) for kb, diag in ((i, True), (jnp.maximum(i - 1, 0), False))]
    hsl = lambda h: slice(h * SB_HEAD_DIM, (h + 1) * SB_HEAD_DIM)
    rowsl = lambda kb: pl.ds(pl.multiple_of(kb * tq, tq), tq)
    zs = [lax.dot_general(q_ref[:, hsl(h)], k_ref[rowsl(kb), hsl(h)], (((1,), (1,)), ((), ())),
                          preferred_element_type=F32) for h, kb, _ in chains]
    sps, sums, laters = [], [], []
    for z, (_, _, diag) in zip(zs, chains):
        sp = jnp.maximum(z, 0.0) + jnp.log2(1.0 + jnp.exp2(-jnp.abs(z)))
        nlk = jnp.where(mask, sp, 0.0) if diag else sp
        sps.append(sp)
        sums.append(jnp.sum(nlk, axis=-1, keepdims=True))
        laters.append(jnp.dot(nlk.astype(BF16), tri, preferred_element_type=F32))
    pvs = []
    for c, (h, kb, diag) in enumerate(chains):
        e = (zs[c] - sps[c]) - laters[c]
        att = jnp.where(mask, jnp.exp2(e), 0.0) if diag else jnp.exp2(e - sums[c - 1])
        pvs.append(jnp.dot(att.astype(BF16), v_ref[rowsl(kb), hsl(h)], preferred_element_type=F32))
    accs = [pvs[2 * h] + jnp.where(has_prev, pvs[2 * h + 1], 0.0) for h in range(hb)]
    runs = [jnp.where(has_prev, sums[2 * h] + sums[2 * h + 1], sums[2 * h]) for h in range(hb)]

    def cond(c):
        return jnp.logical_and(c[0] >= 0, c[1])

    def body(c):
        kb, _, accs, runs = c
        new = [block(h, kb, runs[h], False) for h in range(hb)]
        accs = tuple(a + pv for a, (pv, _) in zip(accs, new))
        runs = tuple(r for _, r in new)
        return kb - 1, live(runs), accs, runs

    _, _, accs, _ = lax.while_loop(cond, body, (i - 2, live(runs), tuple(accs), tuple(runs)))
    for h in range(hb):
        o_ref[:, h * SB_HEAD_DIM:(h + 1) * SB_HEAD_DIM] = accs[h].astype(o_ref.dtype)


def _attention(q, k, v, batch, seq, *, tq, heads_per_step=4):
    M, W = q.shape
    hw = heads_per_step * SB_HEAD_DIM
    nq = seq // tq
    tri = (jnp.arange(tq)[:, None] > jnp.arange(tq)[None, :]).astype(BF16)
    return pl.pallas_call(
        functools.partial(_attn_kernel, tq=tq),
        grid=(batch, W // hw, nq),
        in_specs=[pl.BlockSpec((tq, hw), lambda b, h, i: (b * nq + i, h)),
                  pl.BlockSpec((seq, hw), lambda b, h, i: (b, h)),
                  pl.BlockSpec((seq, hw), lambda b, h, i: (b, h)),
                  pl.BlockSpec((tq, tq), lambda b, h, i: (0, 0))],
        out_specs=pl.BlockSpec((tq, hw), lambda b, h, i: (b * nq + i, h)),
        out_shape=jax.ShapeDtypeStruct((M, W), BF16),
        compiler_params=_params(("arbitrary", "arbitrary", "arbitrary")),
        name="stickbreak_attn",
    )(q, k, v, tri)


def _tile(n, pref):
    t = pref
    while n % t:
        t //= 2
    return t


def _layer(x2d, batch, seq, norm1_w, w_in, conv_ssm_w, conv_ssm_b, dt_bias, a_log, d_skip, ssm_norm_w,
           q_norm_w, k_norm_w, gate_b, w_ssm_out, w_att_out, w_o, norm2_w, w_up, conv_ffn_w,
           conv_ffn_b, w_down):
    M, D = x2d.shape
    d_inner = w_ssm_out.shape[0]
    ssm_heads = dt_bias.shape[0]
    conv_dim = conv_ssm_w.shape[1]
    sb_width = w_att_out.shape[0]
    ffn = w_down.shape[0]
    tm = _tile(seq, 1024)
    tn = 1024

    o_dt = d_inner + conv_dim
    o_q = o_dt + ssm_heads
    assert d_inner % tn == 0 and conv_dim % tn == 0 and sb_width % tn == 0 and o_q % SUBLANES == 0
    w_in_t = jnp.swapaxes(w_in, 0, 1)
    w_dt = jnp.pad(w_in_t[o_dt:o_q], ((0, LANES - ssm_heads), (0, 0))).astype(BF16)
    pad_heads = lambda p: jnp.pad(p, (0, LANES - ssm_heads)).reshape(1, LANES)

    u, dt = _rmsnorm_dt(x2d, norm1_w, w_dt, pad_heads(dt_bias), tm)

    tm_big = _tile(seq, 2048)

    def proj(name, w, start, n, epilogue, extras, dtype, tn=tn, tm=tm_big, scratch=(), park=False):
        return _mm(name, [u], [(w, start, True)], extras, epilogue, [((M, n), dtype, (tm, tn), _tile_map)],
                   tm=tm, tn=tn, n_cols=n, scratch=scratch, park=park)[0]

    zs = proj("proj_z", w_in_t, 0, d_inner, _ep_silu, [], BF16)
    tn_c = 512
    xbc = proj("proj_xbc", w_in_t, d_inner, conv_dim,
               functools.partial(_ep_conv_silu, tiles_per_seq=seq // tm_big),
               [(conv_ssm_w, (SSM_CONV, tn_c), _col_map), (conv_ssm_b.reshape(1, -1), (1, tn_c), _col_map)],
               BF16, tn=tn_c, scratch=[pltpu.VMEM((SUBLANES, tn_c), F32)], park=True)
    scale = math.log2(math.e) / math.sqrt(SB_HEAD_DIM)
    head_w = lambda w: (w.reshape(1, SB_HEAD_DIM), (1, SB_HEAD_DIM), _fixed_map)
    q = proj("proj_q", w_in_t, o_q, sb_width, functools.partial(_ep_head_norm, scale=scale),
             [head_w(q_norm_w)], BF16)
    k = proj("proj_k", w_in_t, o_q + sb_width, sb_width, functools.partial(_ep_head_norm, scale=1.0),
             [head_w(k_norm_w)], BF16)
    v = proj("proj_v", w_in_t, o_q + 2 * sb_width, sb_width, _ep_store, [], BF16)
    gates = proj("proj_gate", w_in_t, o_q + 3 * sb_width, 2 * D, _ep_sigmoid_bias,
                 [(gate_b.reshape(1, -1), (1, tn), _col_map)], BF16)

    y_ssm_n = _ssd(xbc, dt, zs, pad_heads(a_log),
                   jnp.repeat(d_skip, SSM_HEAD_DIM).reshape(1, d_inner), ssm_norm_w, batch, seq)
    att = _attention(q, k, v, batch, seq, tq=_tile(seq, 256))

    tn_s = 512
    tm_s = _tile(seq, 512)
    nbm = D // tn_s
    mixed = _mm("mix", [y_ssm_n, att], [(w_ssm_out, 0, False), (w_att_out, 0, False)],
                [(gates, (tm_s, tn_s), _tile_map), (gates, (tm_s, tn_s), lambda j, i: (i, j + nbm))],
                _ep_mix, [((M, D), BF16, (tm_s, tn_s), _tile_map)], tm=tm_s, tn=tn_s, n_cols=D)[0]

    x1, u2 = _mm("out_proj", [mixed], [(w_o.astype(BF16), 0, False)],
                 [(x2d, (tm_s, D), _tile_map), (norm2_w.reshape(1, D), (1, D), _col_map)],
                 _ep_residual_norm,
                 [((M, D), F32, (tm_s, D), _tile_map), ((M, D), BF16, (tm_s, D), _tile_map)],
                 tm=tm_s, tn=D, n_cols=D)

    hdn = _mm("ffn_up", [u2], [(w_up, 0, False), (w_up, ffn // tn_s, False)],
              [(conv_ffn_w, (FFN_CONV, tn_s), _col_map), (conv_ffn_b.reshape(1, -1), (1, tn_s), _col_map)],
              functools.partial(_ep_ffn_up, tiles_per_seq=seq // tm_big),
              [((M, ffn), BF16, (tm_big, tn_s), _tile_map)], tm=tm_big, tn=tn_s, n_cols=ffn,
              scratch=[pltpu.VMEM((SUBLANES, tn_s), F32)])[0]
    out = _mm("ffn_down", [hdn], [(w_down, 0, False)], [(x1, (tm_s, tn_s), _tile_map)], _ep_residual,
              [((M, D), F32, (tm_s, tn_s), _tile_map)], tm=tm_s, tn=tn_s, n_cols=D)[0]
    return out


def kernel(x, norm1_w, w_in, conv_ssm_w, conv_ssm_b, dt_bias, a_log, d_skip, ssm_norm_w, q_norm_w,
           k_norm_w, gate_b, w_ssm_out, w_att_out, w_o, norm2_w, w_up, conv_ffn_w, conv_ffn_b, w_down):
    b, s, d = x.shape
    x2d = x.reshape(b * s, d)
    for l in range(norm1_w.shape[0]):
        x2d = _layer(x2d, b, s, norm1_w[l], w_in[l], conv_ssm_w[l], conv_ssm_b[l], dt_bias[l], a_log[l],
                     d_skip[l], ssm_norm_w[l], q_norm_w[l], k_norm_w[l], gate_b[l], w_ssm_out[l],
                     w_att_out[l], w_o[l], norm2_w[l], w_up[l], conv_ffn_w[l], conv_ffn_b[l], w_down[l])
    return x2d.reshape(b, s, d)
```
